```python
import math
import jax, jax.numpy as jnp
from jax import lax
import numpy as np

D_MODEL = 1024
BATCH = 16
SEQ = 2048
DEPTH = 1
DEC_BATCH = 16
DEC_SEQ = 4096
PAST_LEN = 128

N_HEADS = 8
QK_NOPE_DIM = 64
QK_ROPE_DIM = 32
QK_DIM = QK_NOPE_DIM + QK_ROPE_DIM
V_HEAD_DIM = 64
Q_LORA_RANK = 256
KV_LORA_RANK = 128
ATTN_WIDTH = N_HEADS * V_HEAD_DIM
ROPE_THETA = 10000.0
Q_BLOCK = 128
LRU_WIDTH = D_MODEL
LRU_BLOCKS = 8
LRU_BLOCK_DIM = LRU_WIDTH // LRU_BLOCKS
CONV_WIDTH = 4
LRU_C = 8.0
N_DIRECTIONS = 2
D_FF = 2816
FFN_RESIDUAL = 0.5
N_SUBLAYERS = 3
EPS = 1e-6
SPLIT_POINTS = (
    Q_LORA_RANK,
    Q_LORA_RANK + KV_LORA_RANK,
    Q_LORA_RANK + KV_LORA_RANK + QK_ROPE_DIM,
    Q_LORA_RANK + KV_LORA_RANK + QK_ROPE_DIM + LRU_WIDTH,
    Q_LORA_RANK + KV_LORA_RANK + QK_ROPE_DIM + 2 * LRU_WIDTH,
    Q_LORA_RANK + KV_LORA_RANK + QK_ROPE_DIM + 2 * LRU_WIDTH + D_MODEL,
)
COMBINED_WIDTH = Q_LORA_RANK + KV_LORA_RANK + QK_ROPE_DIM + 2 * LRU_WIDTH + 2 * D_MODEL

kernel_name = "hybrid_mla_rglru_macaron_encoder"


def rmsnorm(x, g):
    xf = x.astype(jnp.float32)
    y = xf * lax.rsqrt(jnp.mean(xf * xf, axis=-1, keepdims=True) + EPS)
    return (y * g.astype(jnp.float32)).astype(x.dtype)


def rope_tables(seq_len):
    inv = 1.0 / (ROPE_THETA ** (jnp.arange(0, QK_ROPE_DIM, 2, dtype=jnp.float32) / QK_ROPE_DIM))
    ang = jnp.arange(seq_len, dtype=jnp.float32)[:, None] * inv[None, :]
    return jnp.cos(ang), jnp.sin(ang)


def apply_rope(x, cos, sin):
    x1, x2 = jnp.split(x.astype(jnp.float32), 2, axis=-1)
    c = cos[None, :, None, :]
    s = sin[None, :, None, :]
    return jnp.concatenate([x1 * c - x2 * s, x2 * c + x1 * s], axis=-1).astype(x.dtype)


def swiglu(h, w_gu, w_down):
    g, u = jnp.split(h @ w_gu, 2, axis=-1)
    return (jax.nn.silu(g) * u) @ w_down


def bidir_attention(q, k, v):
    b, s = q.shape[0], q.shape[1]
    nb = s // Q_BLOCK
    qb = q.reshape(b, nb, Q_BLOCK, N_HEADS, QK_DIM).transpose(1, 0, 2, 3, 4)
    scale = QK_DIM ** -0.5

    def one_block(qblk):
        sc = jnp.einsum("bqhd,bkhd->bhqk", qblk, k, preferred_element_type=jnp.float32) * scale
        p = jax.nn.softmax(sc, axis=-1)
        return jnp.einsum("bhqk,bkhd->bqhd", p.astype(v.dtype), v)

    o = lax.map(one_block, qb)
    return o.transpose(1, 0, 2, 3, 4).reshape(b, s, ATTN_WIDTH)


def mla_branch(c_q, c_kv, k_rope, cos, sin, g_q_norm, g_kv_norm, w_q_b, w_kv_b, w_attn_o):
    b, s = c_q.shape[0], c_q.shape[1]
    q = (rmsnorm(c_q, g_q_norm) @ w_q_b).reshape(b, s, N_HEADS, QK_DIM)
    q = jnp.concatenate([q[..., :QK_NOPE_DIM], apply_rope(q[..., QK_NOPE_DIM:], cos, sin)], axis=-1)
    kv = (rmsnorm(c_kv, g_kv_norm) @ w_kv_b).reshape(b, s, N_HEADS, QK_NOPE_DIM + V_HEAD_DIM)
    k_nope, v = kv[..., :QK_NOPE_DIM], kv[..., QK_NOPE_DIM:]
    k_r = apply_rope(k_rope[:, :, None, :], cos, sin)
    k = jnp.concatenate([k_nope, jnp.broadcast_to(k_r, (b, s, N_HEADS, QK_ROPE_DIM))], axis=-1)
    return bidir_attention(q, k, v) @ w_attn_o


def centred_depthwise_conv(x, w, bias):
    s = x.shape[1]
    left = (CONV_WIDTH - 1) // 2
    right = CONV_WIDTH - 1 - left
    xp = jnp.pad(x, ((0, 0), (left, right), (0, 0)))
    out = bias + xp[:, 0:s, :] * w[0]
    for j in range(1, CONV_WIDTH):
        out = out + xp[:, j:j + s, :] * w[j]
    return out


def block_diag_linear(x, w, bias):
    b, s = x.shape[0], x.shape[1]
    xb = x.reshape(b, s, LRU_BLOCKS, LRU_BLOCK_DIM)
    return jnp.einsum("bsnd,nde->bsne", xb, w).reshape(b, s, LRU_WIDTH) + bias


def rglru_direction(x, w_a, b_a, w_i, b_i, lam, reverse):
    r = jax.nn.sigmoid(block_diag_linear(x, w_a, b_a).astype(jnp.float32))
    i = jax.nn.sigmoid(block_diag_linear(x, w_i, b_i).astype(jnp.float32))
    log_a = -LRU_C * r * jax.nn.softplus(-lam.astype(jnp.float32))
    a = jnp.exp(log_a)
    u = jnp.sqrt(-jnp.expm1(2.0 * log_a)) * (i * x.astype(jnp.float32))

    def combine(e1, e2):
        a1, b1 = e1
        a2, b2 = e2
        return a1 * a2, a2 * b1 + b2

    _, h = lax.associative_scan(combine, (a, u), axis=1, reverse=reverse)
    return h


def encoder_layer(x, c, cos, sin, w_ada, b_ada, g_pre, g_post, w_ffn1_gu, w_ffn1_down, w_in,
                  g_q_norm, g_kv_norm, w_q_b, w_kv_b, w_attn_o, conv_w, conv_b,
                  lru_w_a, lru_b_a, lru_w_i, lru_b_i, lru_lambda, w_lru_o, w_out,
                  w_ffn2_gu, w_ffn2_down):
    b = x.shape[0]
    mod = (jax.nn.silu(c) @ w_ada + b_ada).reshape(b, N_SUBLAYERS, 3, D_MODEL)
    shift = mod[:, :, 0, None, :]
    scl = mod[:, :, 1, None, :]
    gate = mod[:, :, 2, None, :]

    h = rmsnorm(x, g_pre[0]) * (1.0 + scl[:, 0]) + shift[:, 0]
    f = rmsnorm(swiglu(h, w_ffn1_gu, w_ffn1_down), g_post[0])
    x = x + FFN_RESIDUAL * gate[:, 0] * f

    h = rmsnorm(x, g_pre[1]) * (1.0 + scl[:, 1]) + shift[:, 1]
    z = h @ w_in
    c_q, c_kv, k_rope, x_lru, y_lru, g_att, g_lru = jnp.split(z, SPLIT_POINTS, axis=-1)
    o_att = mla_branch(c_q, c_kv, k_rope, cos, sin, g_q_norm, g_kv_norm, w_q_b, w_kv_b, w_attn_o)
    xc = centred_depthwise_conv(x_lru, conv_w, conv_b)
    h_lru = (rglru_direction(xc, lru_w_a[0], lru_b_a[0], lru_w_i[0], lru_b_i[0], lru_lambda[0], False)
             + rglru_direction(xc, lru_w_a[1], lru_b_a[1], lru_w_i[1], lru_b_i[1], lru_lambda[1], True))
    o_lru = (h_lru.astype(x.dtype) * jax.nn.gelu(y_lru)) @ w_lru_o
    merged = jax.nn.sigmoid(g_att) * o_att + jax.nn.sigmoid(g_lru) * o_lru
    m = rmsnorm(merged @ w_out, g_post[1])
    x = x + gate[:, 1] * m

    h = rmsnorm(x, g_pre[2]) * (1.0 + scl[:, 2]) + shift[:, 2]
    f = rmsnorm(swiglu(h, w_ffn2_gu, w_ffn2_down), g_post[2])
    return x + FFN_RESIDUAL * gate[:, 2] * f


def setup_inputs(seed: int = 0) -> dict:
    key = jax.random.key(seed)
    ks = jax.random.split(key, 32)
    f32 = jnp.float32

    def normal(k, shape, scale):
        return jax.random.normal(k, shape, f32) * scale

    u = jax.random.uniform(ks[24], (DEPTH, N_DIRECTIONS, LRU_WIDTH), f32, minval=0.9, maxval=0.999)
    a_base = u ** (1.0 / LRU_C)
    lru_lambda = jnp.log(a_base) - jnp.log1p(-a_base)
    return {
        "x_prompt": normal(ks[0], (BATCH, SEQ, D_MODEL), 1.0),
        "x_sample": normal(ks[1], (DEC_BATCH, DEC_SEQ, D_MODEL), 1.0),
        "c_prompt": normal(ks[2], (BATCH, D_MODEL), 1.0),
        "c_sample": normal(ks[3], (DEC_BATCH, D_MODEL), 1.0),
        "w_ada": normal(ks[4], (DEPTH, D_MODEL, N_SUBLAYERS * 3 * D_MODEL), 0.5 * D_MODEL ** -0.5),
        "b_ada": normal(ks[5], (DEPTH, N_SUBLAYERS * 3 * D_MODEL), 0.02),
        "g_pre": 1.0 + normal(ks[6], (DEPTH, N_SUBLAYERS, D_MODEL), 0.05),
        "g_post": 1.0 + normal(ks[7], (DEPTH, N_SUBLAYERS, D_MODEL), 0.05),
        "w_ffn1_gu": normal(ks[8], (DEPTH, D_MODEL, 2 * D_FF), D_MODEL ** -0.5),
        "w_ffn1_down": normal(ks[9], (DEPTH, D_FF, D_MODEL), D_FF ** -0.5),
        "w_in": normal(ks[10], (DEPTH, D_MODEL, COMBINED_WIDTH), D_MODEL ** -0.5),
        "g_q_norm": 1.0 + normal(ks[11], (DEPTH, Q_LORA_RANK), 0.05),
        "g_kv_norm": 1.0 + normal(ks[12], (DEPTH, KV_LORA_RANK), 0.05),
        "w_q_b": normal(ks[13], (DEPTH, Q_LORA_RANK, N_HEADS * QK_DIM), Q_LORA_RANK ** -0.5),
        "w_kv_b": normal(ks[14], (DEPTH, KV_LORA_RANK, N_HEADS * (QK_NOPE_DIM + V_HEAD_DIM)), KV_LORA_RANK ** -0.5),
        "w_attn_o": normal(ks[15], (DEPTH, ATTN_WIDTH, D_MODEL), ATTN_WIDTH ** -0.5),
        "conv_w": normal(ks[16], (DEPTH, CONV_WIDTH, LRU_WIDTH), CONV_WIDTH ** -0.5),
        "conv_b": normal(ks[17], (DEPTH, LRU_WIDTH), 0.02),
        "lru_w_a": normal(ks[18], (DEPTH, N_DIRECTIONS, LRU_BLOCKS, LRU_BLOCK_DIM, LRU_BLOCK_DIM), LRU_BLOCK_DIM ** -0.5),
        "lru_b_a": normal(ks[19], (DEPTH, N_DIRECTIONS, LRU_WIDTH), 0.1),
        "lru_w_i": normal(ks[20], (DEPTH, N_DIRECTIONS, LRU_BLOCKS, LRU_BLOCK_DIM, LRU_BLOCK_DIM), LRU_BLOCK_DIM ** -0.5),
        "lru_b_i": normal(ks[21], (DEPTH, N_DIRECTIONS, LRU_WIDTH), 0.1),
        "lru_lambda": lru_lambda,
        "w_lru_o": normal(ks[22], (DEPTH, LRU_WIDTH, D_MODEL), LRU_WIDTH ** -0.5),
        "w_out": normal(ks[23], (DEPTH, D_MODEL, D_MODEL), D_MODEL ** -0.5),
        "w_ffn2_gu": normal(ks[25], (DEPTH, D_MODEL, 2 * D_FF), D_MODEL ** -0.5),
        "w_ffn2_down": normal(ks[26], (DEPTH, D_FF, D_MODEL), D_FF ** -0.5),
    }


def reference(x_prompt, x_sample, c_prompt, c_sample, w_ada, b_ada, g_pre, g_post, w_ffn1_gu,
              w_ffn1_down, w_in, g_q_norm, g_kv_norm, w_q_b, w_kv_b, w_attn_o, conv_w, conv_b,
              lru_w_a, lru_b_a, lru_w_i, lru_b_i, lru_lambda, w_lru_o, w_out, w_ffn2_gu, w_ffn2_down):
    cos_p, sin_p = rope_tables(x_prompt.shape[1])
    cos_s, sin_s = rope_tables(x_sample.shape[1])
    y_prompt = x_prompt
    y_sample = x_sample
    for l in range(DEPTH):
        layer_params = (w_ada[l], b_ada[l], g_pre[l], g_post[l], w_ffn1_gu[l], w_ffn1_down[l], w_in[l],
                        g_q_norm[l], g_kv_norm[l], w_q_b[l], w_kv_b[l], w_attn_o[l], conv_w[l], conv_b[l],
                        lru_w_a[l], lru_b_a[l], lru_w_i[l], lru_b_i[l], lru_lambda[l], w_lru_o[l], w_out[l],
                        w_ffn2_gu[l], w_ffn2_down[l])
        y_prompt = encoder_layer(y_prompt, c_prompt, cos_p, sin_p, *layer_params)
        y_sample = encoder_layer(y_sample, c_sample, cos_s, sin_s, *layer_params)
    return (y_prompt, y_sample)
```

```python
import functools
import math

import jax
import jax.numpy as jnp
from jax import lax
from jax.experimental import pallas as pl
from jax.experimental.pallas import tpu as pltpu

F32 = jnp.float32
BF16 = jnp.bfloat16

D_MODEL = 1024
N_HEADS = 8
QK_NOPE = 64
QK_ROPE = 32
QK_DIM = QK_NOPE + QK_ROPE
V_DIM = 64
Q_LORA = 256
KV_LORA = 128
HEAD_PAD = 128
ROPE_THETA = 10000.0
LRU_BLOCK = 128
CONV_WIDTH = 4
LRU_C = 8.0
D_FF = 2816
EPS = 1e-6
FFN_RESIDUAL = 0.5

FF_CHUNK = 256
TOKEN_TILE = 512
Q_TILE = 512
LRU_CH_TILE = 256
SEGMENTS = 8
SEG_PAD = 8
VMEM_LIMIT = 56 * 1024 * 1024

_C_Q = 0
_C_KV = _C_Q + Q_LORA
_C_KR = _C_KV + KV_LORA
_C_KRP = _C_KR + HEAD_PAD
_C_X = _C_KRP + HEAD_PAD
_C_Y = _C_X + D_MODEL
_C_GA = _C_Y + D_MODEL
_C_GL = _C_GA + D_MODEL
_C_END = _C_GL + D_MODEL


def _params(n_axes):
    return pltpu.CompilerParams(dimension_semantics=("parallel",) * n_axes,
                                vmem_limit_bytes=VMEM_LIMIT)


def _resident(shape):
    nd = len(shape)
    return pl.BlockSpec(shape, lambda *_: (0,) * nd, pipeline_mode=pl.Buffered(1))


def _rms(x, g):
    y = x * lax.rsqrt(jnp.mean(x * x, axis=-1, keepdims=True) + EPS)
    return y * g


def _dot(a, b):
    return jnp.dot(a, b, preferred_element_type=F32)


def _mod_kernel(c_ref, w_ref, b_ref, o_ref):
    c = c_ref[...]
    s = (c * jax.nn.sigmoid(c)).astype(BF16)
    o_ref[...] = _dot(s, w_ref[...].astype(BF16)) + b_ref[...]


def _modulation(c, w_ada, b_ada):
    nb = c.shape[0]
    n = w_ada.shape[1]
    tn = 1024
    return pl.pallas_call(
        _mod_kernel,
        grid=(n // tn,),
        in_specs=[pl.BlockSpec((nb, D_MODEL), lambda j: (0, 0)),
                  pl.BlockSpec((D_MODEL, tn), lambda j: (0, j)),
                  pl.BlockSpec((1, tn), lambda j: (0, j))],
        out_specs=pl.BlockSpec((nb, tn), lambda j: (0, j)),
        out_shape=jax.ShapeDtypeStruct((nb, n), F32),
        compiler_params=_params(1),
        name="mod",
    )(c, w_ada, b_ada.reshape(1, n))


def _swiglu(h, wgu_ref, wd_ref):
    acc = None
    for c in range(0, D_FF, FF_CHUNK):
        g = _dot(h, wgu_ref[:, c:c + FF_CHUNK])
        u = _dot(h, wgu_ref[:, D_FF + c:D_FF + c + FF_CHUNK])
        a = (g * jax.nn.sigmoid(g) * u).astype(BF16)
        d = _dot(a, wd_ref[c:c + FF_CHUNK, :])
        acc = d if acc is None else acc + d
    return acc


def _ffn_half_step(x, mod_ref, sub, gpre_ref, gpost_ref, wgu_ref, wd_ref):
    shift = mod_ref[0, 3 * sub:3 * sub + 1, :]
    scl = mod_ref[0, 3 * sub + 1:3 * sub + 2, :]
    gate = mod_ref[0, 3 * sub + 2:3 * sub + 3, :]
    h = _rms(x, gpre_ref[sub:sub + 1, :]) * (1.0 + scl) + shift
    f = _rms(_swiglu(h.astype(BF16), wgu_ref, wd_ref), gpost_ref[sub:sub + 1, :])
    return x + (FFN_RESIDUAL * gate) * f


def _ffn_kernel(x_ref, mod_ref, gpre_ref, gpost_ref, wgu_ref, wd_ref, o_ref):
    o_ref[...] = _ffn_half_step(x_ref[...], mod_ref, 0, gpre_ref, gpost_ref, wgu_ref, wd_ref)


def _ffn_call(x2d, mod, g_pre, g_post, wgu, wd, seq):
    n = x2d.shape[0]
    tm = TOKEN_TILE
    per_seq = seq // tm
    tok = pl.BlockSpec((tm, D_MODEL), lambda i: (i, 0))
    return pl.pallas_call(
        _ffn_kernel,
        grid=(n // tm,),
        in_specs=[tok,
                  pl.BlockSpec((1, 9, D_MODEL), lambda i: (i // per_seq, 0, 0)),
                  _resident(g_pre.shape), _resident(g_post.shape),
                  _resident(wgu.shape), _resident(wd.shape)],
        out_specs=tok,
        out_shape=jax.ShapeDtypeStruct((n, D_MODEL), F32),
        compiler_params=_params(1),
        name="ffn",
    )(x2d, mod, g_pre, g_post, wgu, wd)


def _inproj_kernel(x_ref, mod_ref, gpre_ref, win_ref, gq_ref, gkv_ref, wq_ref, wk_ref, wv_ref,
                   cos_ref, sin_ref,
                   q_ref, k_ref, v_ref, xl_ref, gy_ref, sa_ref, sl_ref):
    x = x_ref[...]
    shift = mod_ref[0, 3:4, :]
    scl = mod_ref[0, 4:5, :]
    h = (_rms(x, gpre_ref[1:2, :]) * (1.0 + scl) + shift).astype(BF16)
    cos = cos_ref[...]
    sin = sin_ref[...]

    c_q = _dot(h, win_ref[:, _C_Q:_C_KV])
    c_kv = _dot(h, win_ref[:, _C_KV:_C_KR])
    cq_n = _rms(c_q, gq_ref[...]).astype(BF16)
    ckv_n = _rms(c_kv, gkv_ref[...]).astype(BF16)

    k_rope = (_dot(h, win_ref[:, _C_KR:_C_KRP]) * cos + _dot(h, win_ref[:, _C_KRP:_C_X]) * sin)
    scale = QK_DIM ** -0.5
    n_q = N_HEADS * HEAD_PAD
    for hd in range(N_HEADS):
        lo = hd * HEAD_PAD
        q_h = _dot(cq_n, wq_ref[:, lo:lo + HEAD_PAD])
        q_p = _dot(cq_n, wq_ref[:, n_q + lo:n_q + lo + HEAD_PAD])
        q_ref[:, lo:lo + HEAD_PAD] = ((q_h * cos + q_p * sin) * scale).astype(BF16)
        k_h = _dot(ckv_n, wk_ref[:, lo:lo + HEAD_PAD])
        k_ref[:, lo:lo + HEAD_PAD] = (k_h + k_rope).astype(BF16)
    v_ref[...] = _dot(ckv_n, wv_ref[...]).astype(BF16)

    xl_ref[...] = _dot(h, win_ref[:, _C_X:_C_Y]).astype(BF16)
    gy_ref[...] = jax.nn.gelu(_dot(h, win_ref[:, _C_Y:_C_GA])).astype(BF16)
    sa_ref[...] = jax.nn.sigmoid(_dot(h, win_ref[:, _C_GA:_C_GL])).astype(BF16)
    sl_ref[...] = jax.nn.sigmoid(_dot(h, win_ref[:, _C_GL:_C_END])).astype(BF16)


def _inproj_call(x2d, mod, g_pre, win, gq, gkv, wq, wk, wv, cos_t, sin_t, seq):
    n = x2d.shape[0]
    tm = TOKEN_TILE
    per_seq = seq // tm
    tok = lambda w: pl.BlockSpec((tm, w), lambda i: (i, 0))
    pos = pl.BlockSpec((tm, HEAD_PAD), lambda i: (i % per_seq, 0))
    wide = N_HEADS * HEAD_PAD
    outs = [(wide, BF16), (wide, BF16), (N_HEADS * V_DIM, BF16)] + [(D_MODEL, BF16)] * 4
    return pl.pallas_call(
        _inproj_kernel,
        grid=(n // tm,),
        in_specs=[tok(D_MODEL),
                  pl.BlockSpec((1, 9, D_MODEL), lambda i: (i // per_seq, 0, 0)),
                  _resident(g_pre.shape), _resident(win.shape), _resident(gq.shape),
                  _resident(gkv.shape), _resident(wq.shape), _resident(wk.shape),
                  _resident(wv.shape), pos, pos],
        out_specs=[tok(w) for w, _ in outs],
        out_shape=[jax.ShapeDtypeStruct((n, w), dt) for w, dt in outs],
        compiler_params=_params(1),
        name="inproj",
    )(x2d, mod, g_pre, win, gq, gkv, wq, wk, wv, cos_t, sin_t)


def _attn_kernel(q_ref, k_ref, v_ref, o_ref):
    v_pair = v_ref[0]
    outs = []
    for j in range(2):
        q = q_ref[0, :, j * HEAD_PAD:(j + 1) * HEAD_PAD]
        k = k_ref[0, :, j * HEAD_PAD:(j + 1) * HEAD_PAD]
        s = lax.dot_general(q, k, (((1,), (1,)), ((), ())), preferred_element_type=F32)
        m = jnp.max(s, axis=-1, keepdims=True)
        p = jnp.exp(s - m)
        l = jnp.sum(p, axis=-1, keepdims=True)
        outs.append(_dot(p.astype(BF16), v_pair) / l)
    lane = lax.broadcasted_iota(jnp.int32, outs[0].shape, 1)
    o_ref[0] = jnp.where(lane < V_DIM, outs[0], outs[1]).astype(BF16)


def _attn_call(q, k, v):
    b, s, _ = q.shape
    tq = Q_TILE
    return pl.pallas_call(
        _attn_kernel,
        grid=(b, N_HEADS // 2, s // tq),
        in_specs=[pl.BlockSpec((1, tq, 2 * HEAD_PAD), lambda bi, p, i: (bi, i, p)),
                  pl.BlockSpec((1, s, 2 * HEAD_PAD), lambda bi, p, i: (bi, 0, p)),
                  pl.BlockSpec((1, s, 2 * V_DIM), lambda bi, p, i: (bi, 0, p))],
        out_specs=pl.BlockSpec((1, tq, 2 * V_DIM), lambda bi, p, i: (bi, i, p)),
        out_shape=jax.ShapeDtypeStruct((b, s, N_HEADS * V_DIM), BF16),
        compiler_params=_params(3),
        name="attn",
    )(q, k, v)


def _lru_kernel(x_ref, gy_ref, cw_ref, cb_ref, wg_ref, bg_ref, lam_ref, o_ref,
                xp_scr, a_scr, u_scr, *, seq):
    n_slab = LRU_CH_TILE // LRU_BLOCK
    seg = seq // SEGMENTS
    pitch = seg + SEG_PAD
    halo = 8

    zeros = jnp.zeros((halo, LRU_CH_TILE), F32)
    xp_scr[0:halo, :] = zeros
    xp_scr[halo + seq:halo + seq + halo, :] = zeros
    xp_scr[halo:halo + seq, :] = x_ref[0].astype(F32)

    lam = lam_ref[...]
    neg_lam = -lam
    softplus = jnp.maximum(neg_lam, 0.0) + jnp.log1p(jnp.exp(-jnp.abs(neg_lam)))
    decay = -LRU_C * softplus

    for sg in range(SEGMENTS):
        r0 = halo + sg * seg
        xc = cb_ref[...] + xp_scr[r0 - 1:r0 - 1 + seg, :] * cw_ref[0:1, :]
        for t in range(1, CONV_WIDTH):
            xc = xc + xp_scr[r0 - 1 + t:r0 - 1 + t + seg, :] * cw_ref[t:t + 1, :]
        for sl in range(n_slab):
            lanes = slice(sl * LRU_BLOCK, (sl + 1) * LRU_BLOCK)
            x_s = xc[:, lanes]
            g = _dot(x_s.astype(BF16), wg_ref[sl]) + bg_ref[sl]
            for d in range(2):
                r = jax.nn.sigmoid(g[:, (2 * d) * LRU_BLOCK:(2 * d + 1) * LRU_BLOCK])
                i = jax.nn.sigmoid(g[:, (2 * d + 1) * LRU_BLOCK:(2 * d + 2) * LRU_BLOCK])
                log_a = decay[d:d + 1, lanes] * r
                a = jnp.exp(log_a)
                u = jnp.sqrt(-jnp.tanh(log_a) * (1.0 + a * a)) * (i * x_s)
                ch = d * n_slab + sl
                a_scr[ch, sg * pitch:sg * pitch + seg, :] = a
                u_scr[ch, sg * pitch:sg * pitch + seg, :] = u

    n_ch = 2 * n_slab

    def rows(ch, j):
        idx = j if ch < n_slab else seg - 1 - j
        return pl.ds(idx, SEGMENTS, stride=pitch)

    def summarize(j, carry):
        out = []
        for ch in range(n_ch):
            h, p = carry[ch]
            a = a_scr[ch, rows(ch, j), :]
            u = u_scr[ch, rows(ch, j), :]
            out.append((a * h + u, p * a))
        return tuple(out)

    init = tuple((jnp.zeros((SEGMENTS, LRU_BLOCK), F32), jnp.ones((SEGMENTS, LRU_BLOCK), F32))
                 for _ in range(n_ch))
    ends = lax.fori_loop(0, seg, summarize, init)

    starts = []
    for ch in range(n_ch):
        h_end, p_end = ends[ch]
        order = range(SEGMENTS) if ch < n_slab else range(SEGMENTS - 1, -1, -1)
        c = jnp.zeros((1, LRU_BLOCK), F32)
        by_seg = {}
        for sg in order:
            by_seg[sg] = c
            c = h_end[sg:sg + 1, :] + p_end[sg:sg + 1, :] * c
        starts.append(jnp.concatenate([by_seg[sg] for sg in range(SEGMENTS)], axis=0))

    def replay(j, carry):
        out = []
        for ch in range(n_ch):
            a = a_scr[ch, rows(ch, j), :]
            u = u_scr[ch, rows(ch, j), :]
            h = a * carry[ch] + u
            u_scr[ch, rows(ch, j), :] = h
            out.append(h)
        return tuple(out)

    lax.fori_loop(0, seg, replay, tuple(starts))

    for sg in range(SEGMENTS):
        for sl in range(n_slab):
            lanes = slice(sl * LRU_BLOCK, (sl + 1) * LRU_BLOCK)
            span = slice(sg * pitch, sg * pitch + seg)
            h = u_scr[sl, span, :] + u_scr[n_slab + sl, span, :]
            gy = gy_ref[0, sg * seg:(sg + 1) * seg, lanes].astype(F32)
            o_ref[0, sg * seg:(sg + 1) * seg, lanes] = (h * gy).astype(BF16)


def _lru_call(xl, gy, conv_w, conv_b, wg, bg, lam):
    b, s, _ = xl.shape
    ct = LRU_CH_TILE
    n_slab = ct // LRU_BLOCK
    pitch = s // SEGMENTS + SEG_PAD
    seq_blk = pl.BlockSpec((1, s, ct), lambda bi, c: (bi, 0, c))
    return pl.pallas_call(
        functools.partial(_lru_kernel, seq=s),
        grid=(b, D_MODEL // ct),
        in_specs=[seq_blk, seq_blk,
                  pl.BlockSpec((CONV_WIDTH, ct), lambda bi, c: (0, c)),
                  pl.BlockSpec((1, ct), lambda bi, c: (0, c)),
                  pl.BlockSpec((n_slab, LRU_BLOCK, 4 * LRU_BLOCK), lambda bi, c: (c, 0, 0)),
                  pl.BlockSpec((n_slab, 1, 4 * LRU_BLOCK), lambda bi, c: (c, 0, 0)),
                  pl.BlockSpec((2, ct), lambda bi, c: (0, c))],
        out_specs=seq_blk,
        out_shape=jax.ShapeDtypeStruct((b, s, D_MODEL), BF16),
        scratch_shapes=[pltpu.VMEM((s + 16, ct), F32),
                        pltpu.VMEM((2 * n_slab, SEGMENTS * pitch, LRU_BLOCK), F32),
                        pltpu.VMEM((2 * n_slab, SEGMENTS * pitch, LRU_BLOCK), F32)],
        compiler_params=_params(2),
        name="lru",
    )(xl, gy, conv_w, conv_b, wg, bg, lam)


def _merge_kernel(x_ref, att_ref, hl_ref, sa_ref, sl_ref, mod_ref, gpre_ref, gpost_ref,
                  wao_ref, wlo_ref, wout_ref, wgu_ref, wd_ref, o_ref):
    o_att = _dot(att_ref[...], wao_ref[...])
    o_lru = _dot(hl_ref[...], wlo_ref[...])
    merged = sa_ref[...].astype(F32) * o_att + sl_ref[...].astype(F32) * o_lru
    m = _rms(_dot(merged.astype(BF16), wout_ref[...]), gpost_ref[1:2, :])
    x = x_ref[...] + mod_ref[0, 5:6, :] * m
    o_ref[...] = _ffn_half_step(x, mod_ref, 2, gpre_ref, gpost_ref, wgu_ref, wd_ref)


def _merge_call(x2d, att, hl, sa, sl, mod, g_pre, g_post, wao, wlo, wout, wgu, wd, seq):
    n = x2d.shape[0]
    tm = TOKEN_TILE
    per_seq = seq // tm
    tok = lambda w: pl.BlockSpec((tm, w), lambda i: (i, 0))
    return pl.pallas_call(
        _merge_kernel,
        grid=(n // tm,),
        in_specs=[tok(D_MODEL), tok(N_HEADS * V_DIM), tok(D_MODEL), tok(D_MODEL), tok(D_MODEL),
                  pl.BlockSpec((1, 9, D_MODEL), lambda i: (i // per_seq, 0, 0)),
                  _resident(g_pre.shape), _resident(g_post.shape), _resident(wao.shape),
                  _resident(wlo.shape), _resident(wout.shape), _resident(wgu.shape),
                  _resident(wd.shape)],
        out_specs=tok(D_MODEL),
        out_shape=jax.ShapeDtypeStruct((n, D_MODEL), F32),
        compiler_params=_params(1),
        name="merge",
    )(x2d, att, hl, sa, sl, mod, g_pre, g_post, wao, wlo, wout, wgu, wd)


def _rope_partner(w):
    half = QK_ROPE // 2
    return jnp.concatenate([-w[:, half:], w[:, :half]], axis=1)


def _pad_rope_cols(w):
    rows = w.shape[0]
    return jnp.concatenate([jnp.zeros((rows, QK_NOPE), w.dtype), w,
                            jnp.zeros((rows, HEAD_PAD - QK_DIM), w.dtype)], axis=1)


def _prep_layer(w_in, w_q_b, w_kv_b, lru_w_a, lru_b_a, lru_w_i, lru_b_i):
    q0, q1, q2, q3, q4, q5 = (Q_LORA, Q_LORA + KV_LORA, Q_LORA + KV_LORA + QK_ROPE,
                              Q_LORA + KV_LORA + QK_ROPE + D_MODEL,
                              Q_LORA + KV_LORA + QK_ROPE + 2 * D_MODEL,
                              Q_LORA + KV_LORA + QK_ROPE + 3 * D_MODEL)
    w_kr = w_in[:, q1:q2]
    win = jnp.concatenate([w_in[:, :q1], _pad_rope_cols(w_kr), _pad_rope_cols(_rope_partner(w_kr)),
                           w_in[:, q2:q3], w_in[:, q3:q4], w_in[:, q4:q5], w_in[:, q5:]],
                          axis=1).astype(BF16)

    wq = w_q_b.reshape(Q_LORA, N_HEADS, QK_DIM)
    pad = jnp.zeros((Q_LORA, N_HEADS, HEAD_PAD - QK_DIM), w_q_b.dtype)
    zero_nope = jnp.zeros((Q_LORA, N_HEADS, QK_NOPE), w_q_b.dtype)
    rope = wq[:, :, QK_NOPE:]
    half = QK_ROPE // 2
    partner = jnp.concatenate([-rope[:, :, half:], rope[:, :, :half]], axis=2)
    wq_main = jnp.concatenate([wq, pad], axis=2).reshape(Q_LORA, N_HEADS * HEAD_PAD)
    wq_part = jnp.concatenate([zero_nope, partner, pad], axis=2).reshape(Q_LORA, N_HEADS * HEAD_PAD)
    wq2 = jnp.concatenate([wq_main, wq_part], axis=1).astype(BF16)

    wkv = w_kv_b.reshape(KV_LORA, N_HEADS, QK_NOPE + V_DIM)
    wk = jnp.concatenate([wkv[:, :, :QK_NOPE],
                          jnp.zeros((KV_LORA, N_HEADS, HEAD_PAD - QK_NOPE), w_kv_b.dtype)],
                         axis=2).reshape(KV_LORA, N_HEADS * HEAD_PAD).astype(BF16)
    wv = wkv[:, :, QK_NOPE:].reshape(KV_LORA, N_HEADS * V_DIM).astype(BF16)

    wg = jnp.concatenate([lru_w_a[0], lru_w_i[0], lru_w_a[1], lru_w_i[1]], axis=2).astype(BF16)
    nblk = D_MODEL // LRU_BLOCK
    bg = jnp.concatenate([lru_b_a[0].reshape(nblk, 1, LRU_BLOCK), lru_b_i[0].reshape(nblk, 1, LRU_BLOCK),
                          lru_b_a[1].reshape(nblk, 1, LRU_BLOCK), lru_b_i[1].reshape(nblk, 1, LRU_BLOCK)],
                         axis=2)
    return win, wq2, wk, wv, wg, bg


def _rope_tables(seq):
    inv = 1.0 / (ROPE_THETA ** (jnp.arange(0, QK_ROPE, 2, dtype=F32) / QK_ROPE))
    ang = jnp.arange(seq, dtype=F32)[:, None] * inv[None, :]
    cos, sin = jnp.cos(ang), jnp.sin(ang)
    ones = jnp.ones((seq, QK_NOPE), F32)
    tail1 = jnp.ones((seq, HEAD_PAD - QK_DIM), F32)
    cos_t = jnp.concatenate([ones, cos, cos, tail1], axis=1)
    sin_t = jnp.concatenate([0.0 * ones, sin, sin, 0.0 * tail1], axis=1)
    return cos_t, sin_t


def _encoder_layer(x, mod, lw):
    b, s, _ = x.shape
    x2d = x.reshape(b * s, D_MODEL)
    cos_t, sin_t = _rope_tables(s)
    x1 = _ffn_call(x2d, mod, lw["g_pre"], lw["g_post"], lw["wgu1"], lw["wd1"], s)
    q, k, v, xl, gy, sa, sl = _inproj_call(x1, mod, lw["g_pre"], lw["win"], lw["gq"], lw["gkv"],
                                           lw["wq2"], lw["wk"], lw["wv"], cos_t, sin_t, s)
    att = _attn_call(q.reshape(b, s, -1), k.reshape(b, s, -1), v.reshape(b, s, -1))
    hl = _lru_call(xl.reshape(b, s, D_MODEL), gy.reshape(b, s, D_MODEL), lw["conv_w"], lw["conv_b"],
                   lw["wg"], lw["bg"], lw["lam"])
    y = _merge_call(x1, att.reshape(b * s, -1), hl.reshape(b * s, D_MODEL), sa, sl, mod,
                    lw["g_pre"], lw["g_post"], lw["wao"], lw["wlo"], lw["wout"], lw["wgu2"],
                    lw["wd2"], s)
    return y.reshape(b, s, D_MODEL)


def kernel(x_prompt, x_sample, c_prompt, c_sample, w_ada, b_ada, g_pre, g_post, w_ffn1_gu, w_ffn1_down, w_in, g_q_norm, g_kv_norm, w_q_b, w_kv_b, w_attn_o, conv_w, conv_b, lru_w_a, lru_b_a, lru_w_i, lru_b_i, lru_lambda, w_lru_o, w_out, w_ffn2_gu, w_ffn2_down):
    depth = w_ada.shape[0]
    n_prompt = c_prompt.shape[0]
    y_prompt, y_sample = x_prompt, x_sample
    for l in range(depth):
        win, wq2, wk, wv, wg, bg = _prep_layer(w_in[l], w_q_b[l], w_kv_b[l], lru_w_a[l], lru_b_a[l],
                                               lru_w_i[l], lru_b_i[l])
        lw = dict(g_pre=g_pre[l], g_post=g_post[l],
                  wgu1=w_ffn1_gu[l].astype(BF16), wd1=w_ffn1_down[l].astype(BF16),
                  win=win, gq=g_q_norm[l].reshape(1, Q_LORA), gkv=g_kv_norm[l].reshape(1, KV_LORA),
                  wq2=wq2, wk=wk, wv=wv, conv_w=conv_w[l], conv_b=conv_b[l].reshape(1, D_MODEL),
                  wg=wg, bg=bg, lam=lru_lambda[l],
                  wao=w_attn_o[l].astype(BF16), wlo=w_lru_o[l].astype(BF16),
                  wout=w_out[l].astype(BF16),
                  wgu2=w_ffn2_gu[l].astype(BF16), wd2=w_ffn2_down[l].astype(BF16))
        c_all = jnp.concatenate([c_prompt, c_sample], axis=0)
        mod = _modulation(c_all, w_ada[l], b_ada[l]).reshape(c_all.shape[0], 9, D_MODEL)
        y_prompt = _encoder_layer(y_prompt, mod[:n_prompt], lw)
        y_sample = _encoder_layer(y_sample, mod[n_prompt:], lw)
    return (y_prompt, y_sample)
```

```python
import functools
import math

import jax
import jax.numpy as jnp
from jax import lax
from jax.experimental import pallas as pl
from jax.experimental.pallas import tpu as pltpu

F32 = jnp.float32
BF16 = jnp.bfloat16

D_MODEL = 1024
N_HEADS = 8
QK_NOPE = 64
QK_ROPE = 32
QK_DIM = QK_NOPE + QK_ROPE
V_DIM = 64
Q_LORA = 256
KV_LORA = 128
HEAD_PAD = 128
ROPE_THETA = 10000.0
LRU_BLOCK = 128
CONV_WIDTH = 4
LRU_C = 8.0
D_FF = 2816
EPS = 1e-6
FFN_RESIDUAL = 0.5

FF_CHUNK = 256
TOKEN_TILE = 512
Q_TILE = 512
LRU_CH_TILE = 256
SEGMENTS = 8
SEG_PAD = 8
SCAN_UNROLL = 8
LRU_STEP_CHUNK = 64
VMEM_LIMIT = 56 * 1024 * 1024

_C_Q = 0
_C_KV = _C_Q + Q_LORA
_C_KR = _C_KV + KV_LORA
_C_KRP = _C_KR + HEAD_PAD
_C_X = _C_KRP + HEAD_PAD
_C_Y = _C_X + D_MODEL
_C_GA = _C_Y + D_MODEL
_C_GL = _C_GA + D_MODEL
_C_END = _C_GL + D_MODEL


def _params(n_axes):
    return pltpu.CompilerParams(dimension_semantics=("parallel",) * n_axes,
                                vmem_limit_bytes=VMEM_LIMIT)


def _resident(shape):
    nd = len(shape)
    return pl.BlockSpec(shape, lambda *_: (0,) * nd, pipeline_mode=pl.Buffered(1))


def _rms(x, g):
    y = x * lax.rsqrt(jnp.mean(x * x, axis=-1, keepdims=True) + EPS)
    return y * g


def _dot(a, b):
    return jnp.dot(a, b, preferred_element_type=F32)


def _dot_nt(a, b):
    return lax.dot_general(a, b, (((1,), (1,)), ((), ())), preferred_element_type=F32)


def _sigmoid(x):
    return 0.5 * (1.0 + jnp.tanh(0.5 * x))


def _mod_kernel(c_ref, w_ref, b_ref, o_ref):
    c = c_ref[...]
    s = (c * _sigmoid(c)).astype(BF16)
    o_ref[...] = _dot(s, w_ref[...].astype(BF16)) + b_ref[...]


def _modulation(c, w_ada, b_ada):
    nb = c.shape[0]
    n = w_ada.shape[1]
    tn = 1024
    return pl.pallas_call(
        _mod_kernel,
        grid=(n // tn,),
        in_specs=[pl.BlockSpec((nb, D_MODEL), lambda j: (0, 0)),
                  pl.BlockSpec((D_MODEL, tn), lambda j: (0, j)),
                  pl.BlockSpec((1, tn), lambda j: (0, j))],
        out_specs=pl.BlockSpec((nb, tn), lambda j: (0, j)),
        out_shape=jax.ShapeDtypeStruct((nb, n), F32),
        compiler_params=_params(1),
        name="mod",
    )(c, w_ada, b_ada.reshape(1, n))


def _swiglu(h, wgu_ref, wd_ref):
    acc = None
    for c in range(0, D_FF, FF_CHUNK):
        g = _dot(h, wgu_ref[:, c:c + FF_CHUNK])
        u = _dot(h, wgu_ref[:, D_FF + c:D_FF + c + FF_CHUNK])
        a = (g * _sigmoid(g) * u).astype(BF16)
        d = _dot(a, wd_ref[c:c + FF_CHUNK, :])
        acc = d if acc is None else acc + d
    return acc


def _ffn_half_step(x, mod_ref, sub, gpre_ref, gpost_ref, wgu_ref, wd_ref):
    shift = mod_ref[0, 3 * sub:3 * sub + 1, :]
    scl = mod_ref[0, 3 * sub + 1:3 * sub + 2, :]
    gate = mod_ref[0, 3 * sub + 2:3 * sub + 3, :]
    h = _rms(x, gpre_ref[sub:sub + 1, :]) * (1.0 + scl) + shift
    f = _rms(_swiglu(h.astype(BF16), wgu_ref, wd_ref), gpost_ref[sub:sub + 1, :])
    return x + (FFN_RESIDUAL * gate) * f


def _ffn_kernel(x_ref, mod_ref, gpre_ref, gpost_ref, wgu_ref, wd_ref, o_ref):
    o_ref[...] = _ffn_half_step(x_ref[...], mod_ref, 0, gpre_ref, gpost_ref, wgu_ref, wd_ref)


def _ffn_call(x2d, mod, g_pre, g_post, wgu, wd, seq):
    n = x2d.shape[0]
    tm = TOKEN_TILE
    per_seq = seq // tm
    tok = pl.BlockSpec((tm, D_MODEL), lambda i: (i, 0))
    return pl.pallas_call(
        _ffn_kernel,
        grid=(n // tm,),
        in_specs=[tok,
                  pl.BlockSpec((1, 9, D_MODEL), lambda i: (i // per_seq, 0, 0)),
                  _resident(g_pre.shape), _resident(g_post.shape),
                  _resident(wgu.shape), _resident(wd.shape)],
        out_specs=tok,
        out_shape=jax.ShapeDtypeStruct((n, D_MODEL), F32),
        compiler_params=_params(1),
        name="ffn",
    )(x2d, mod, g_pre, g_post, wgu, wd)


def _inproj_kernel(x_ref, mod_ref, gpre_ref, win_ref, gq_ref, gkv_ref, wq_ref, wk_ref, wv_ref,
                   cos_ref, sin_ref,
                   q_ref, k_ref, vt_ref, xl_ref, gy_ref, sa_ref, sl_ref):
    x = x_ref[...]
    shift = mod_ref[0, 3:4, :]
    scl = mod_ref[0, 4:5, :]
    h = (_rms(x, gpre_ref[1:2, :]) * (1.0 + scl) + shift).astype(BF16)
    cos = cos_ref[...]
    sin = sin_ref[...]

    lat = _dot(h, win_ref[:, _C_Q:_C_X])
    cq_n = _rms(lat[:, _C_Q:_C_KV], gq_ref[...]).astype(BF16)
    ckv_n = _rms(lat[:, _C_KV:_C_KR], gkv_ref[...]).astype(BF16)

    k_rope = lat[:, _C_KR:_C_KRP] * cos + lat[:, _C_KRP:_C_X] * sin
    scale = QK_DIM ** -0.5 * math.log2(math.e)
    n_q = N_HEADS * HEAD_PAD
    q_all = _dot(cq_n, wq_ref[...])
    k_all = _dot(ckv_n, wk_ref[...])
    for hd in range(N_HEADS):
        lo = hd * HEAD_PAD
        q_h = q_all[:, lo:lo + HEAD_PAD]
        q_p = q_all[:, n_q + lo:n_q + lo + HEAD_PAD]
        q_ref[:, lo:lo + HEAD_PAD] = ((q_h * cos + q_p * sin) * scale).astype(BF16)
        k_ref[:, lo:lo + HEAD_PAD] = (k_all[:, lo:lo + HEAD_PAD] + k_rope).astype(BF16)
    vt_ref[0] = _dot_nt(wv_ref[...], ckv_n).astype(BF16)

    xl_ref[...] = _dot(h, win_ref[:, _C_X:_C_Y]).astype(BF16)
    gy_ref[...] = jax.nn.gelu(_dot(h, win_ref[:, _C_Y:_C_GA])).astype(BF16)
    sa_ref[...] = _sigmoid(_dot(h, win_ref[:, _C_GA:_C_GL])).astype(BF16)
    sl_ref[...] = _sigmoid(_dot(h, win_ref[:, _C_GL:_C_END])).astype(BF16)


def _inproj_call(x2d, mod, g_pre, win, gq, gkv, wq, wk, wv, cos_t, sin_t, seq):
    n = x2d.shape[0]
    tm = TOKEN_TILE
    per_seq = seq // tm
    tok = lambda w: pl.BlockSpec((tm, w), lambda i: (i, 0))
    pos = pl.BlockSpec((tm, HEAD_PAD), lambda i: (i % per_seq, 0))
    wide = N_HEADS * HEAD_PAD
    n_v = N_HEADS * V_DIM
    rows = lambda w: (pl.BlockSpec((tm, w), lambda i: (i, 0)), jax.ShapeDtypeStruct((n, w), BF16))
    v_t = (pl.BlockSpec((1, n_v, tm), lambda i: (i // per_seq, 0, i % per_seq)),
           jax.ShapeDtypeStruct((n // seq, n_v, seq), BF16))
    outs = [rows(wide), rows(wide), v_t] + [rows(D_MODEL)] * 4
    return pl.pallas_call(
        _inproj_kernel,
        grid=(n // tm,),
        in_specs=[tok(D_MODEL),
                  pl.BlockSpec((1, 9, D_MODEL), lambda i: (i // per_seq, 0, 0)),
                  _resident(g_pre.shape), _resident(win.shape), _resident(gq.shape),
                  _resident(gkv.shape), _resident(wq.shape), _resident(wk.shape),
                  _resident(wv.shape), pos, pos],
        out_specs=[spec for spec, _ in outs],
        out_shape=[shape for _, shape in outs],
        compiler_params=_params(1),
        name="inproj",
    )(x2d, mod, g_pre, win, gq, gkv, wq, wk, wv, cos_t, sin_t)


def _attn_kernel(q_ref, k_ref, vt_ref, o_ref):
    def scores(hd):
        lo = hd * HEAD_PAD
        q = q_ref[0, :, lo:lo + HEAD_PAD]
        k = k_ref[0, :, lo:lo + HEAD_PAD]
        return _dot_nt(k, q)

    halves = []
    st_next = scores(0)
    for hd in range(N_HEADS):
        st = st_next
        if hd + 1 < N_HEADS:
            st_next = scores(hd + 1)
        m = jnp.max(st, axis=0, keepdims=True)
        p = jnp.exp2(st - m)
        l = jnp.sum(p, axis=0, keepdims=True)
        vt = vt_ref[0, hd * V_DIM:(hd + 1) * V_DIM, :]
        halves.append(_dot(vt, p.astype(BF16)) / l)
        if hd % 2 == 1:
            pair = hd // 2
            o_pair = jnp.concatenate(halves, axis=0).T
            o_ref[0, :, pair * 2 * V_DIM:(pair + 1) * 2 * V_DIM] = o_pair.astype(BF16)
            halves = []


def _attn_call(q, k, vt):
    b, s, _ = q.shape
    tq = Q_TILE
    return pl.pallas_call(
        _attn_kernel,
        grid=(b, s // tq),
        in_specs=[pl.BlockSpec((1, tq, N_HEADS * HEAD_PAD), lambda bi, i: (bi, i, 0)),
                  pl.BlockSpec((1, s, N_HEADS * HEAD_PAD), lambda bi, i: (bi, 0, 0),
                               pipeline_mode=pl.Buffered(1)),
                  pl.BlockSpec((1, N_HEADS * V_DIM, s), lambda bi, i: (bi, 0, 0),
                               pipeline_mode=pl.Buffered(1))],
        out_specs=pl.BlockSpec((1, tq, N_HEADS * V_DIM), lambda bi, i: (bi, i, 0)),
        out_shape=jax.ShapeDtypeStruct((b, s, N_HEADS * V_DIM), BF16),
        compiler_params=_params(2),
        name="attn",
    )(q, k, vt)


def _lru_kernel(x_ref, gy_ref, cw_ref, cb_ref, wg_ref, bg_ref, lam_ref, o_ref,
                xs_scr, a_scr, u_scr, h_scr, *, seq):
    n_slab = LRU_CH_TILE // LRU_BLOCK
    n_ch = 2 * n_slab
    seg = seq // SEGMENTS
    pitch = seg + SEG_PAD
    tile = (SEGMENTS, LRU_BLOCK)

    def lanes(sl):
        return slice(sl * LRU_BLOCK, (sl + 1) * LRU_BLOCK)

    for sg in range(SEGMENTS):
        for sl in range(n_slab):
            xs_scr[sl, sg * pitch:sg * pitch + seg, :] = (
                x_ref[0, sg * seg:(sg + 1) * seg, lanes(sl)].astype(F32))

    lam = lam_ref[...]
    neg_lam = -lam
    softplus = jnp.maximum(neg_lam, 0.0) + jnp.log1p(jnp.exp(-jnp.abs(neg_lam)))
    decay = -LRU_C * softplus

    row = lax.broadcasted_iota(jnp.int32, tile, 0)

    def step_tile(sl, j):
        if j < 0:
            return jnp.where(row == 0, 0.0, pltpu.roll(step_tile(sl, seg + j), 1, axis=0))
        if j >= seg:
            return jnp.where(row == SEGMENTS - 1, 0.0,
                             pltpu.roll(step_tile(sl, j - seg), SEGMENTS - 1, axis=0))
        return xs_scr[sl, pl.ds(j, SEGMENTS, stride=pitch), :]

    left = (CONV_WIDTH - 1) // 2
    for c0 in range(0, seg, LRU_STEP_CHUNK):
        for sl in range(n_slab):
            tiles = {j: step_tile(sl, j)
                     for j in range(c0 - left, c0 + LRU_STEP_CHUNK + CONV_WIDTH - 1 - left)}
            xc = cb_ref[:, lanes(sl)]
            for t in range(CONV_WIDTH):
                tap = jnp.concatenate([tiles[j + t - left] for j in range(c0, c0 + LRU_STEP_CHUNK)],
                                      axis=0)
                xc = xc + tap * cw_ref[t:t + 1, lanes(sl)]
            g = _dot(xc.astype(BF16), wg_ref[sl]) + bg_ref[sl]
            half_x = 0.5 * xc
            span = slice(c0 * SEGMENTS, (c0 + LRU_STEP_CHUNK) * SEGMENTS)
            for d in range(2):
                t_r = jnp.tanh(g[:, (2 * d) * LRU_BLOCK:(2 * d + 1) * LRU_BLOCK])
                t_i = jnp.tanh(g[:, (2 * d + 1) * LRU_BLOCK:(2 * d + 2) * LRU_BLOCK])
                half_decay = 0.5 * decay[d:d + 1, lanes(sl)]
                log_a = half_decay + half_decay * t_r
                a = jnp.exp(log_a)
                w = -jnp.tanh(log_a) * (1.0 + a * a)
                root = jnp.where(w > 0.0, w * lax.rsqrt(w), 0.0)
                a_scr[d * n_slab + sl, span, :] = a
                u_scr[d * n_slab + sl, span, :] = root * (half_x + half_x * t_i)

    def rows(ch, j):
        idx = j if ch < n_slab else seg - 1 - j
        return pl.ds(pl.multiple_of(idx * SEGMENTS, SEGMENTS), SEGMENTS)

    def summarize(j, carry):
        out = []
        for ch in range(n_ch):
            h, p = carry[ch]
            a = a_scr[ch, rows(ch, j), :]
            u = u_scr[ch, rows(ch, j), :]
            out.append((a * h + u, p * a))
        return tuple(out)

    init = tuple((jnp.zeros(tile, F32), jnp.ones(tile, F32)) for _ in range(n_ch))
    ends = lax.fori_loop(0, seg, summarize, init, unroll=SCAN_UNROLL)

    starts = []
    for ch in range(n_ch):
        h_end, p_end = ends[ch]
        order = range(SEGMENTS) if ch < n_slab else range(SEGMENTS - 1, -1, -1)
        c = jnp.zeros((1, LRU_BLOCK), F32)
        by_seg = {}
        for sg in order:
            by_seg[sg] = c
            c = h_end[sg:sg + 1, :] + p_end[sg:sg + 1, :] * c
        starts.append(jnp.concatenate([by_seg[sg] for sg in range(SEGMENTS)], axis=0))

    def replay(j, carry):
        out = []
        for ch in range(n_ch):
            h = a_scr[ch, rows(ch, j), :] * carry[ch] + u_scr[ch, rows(ch, j), :]
            h_scr[ch, rows(ch, j), :] = h
            out.append(h)
        return tuple(out)

    lax.fori_loop(0, seg, replay, tuple(starts), unroll=SCAN_UNROLL)

    def scatter(j, carry):
        at = pl.ds(pl.multiple_of(j * SEGMENTS, SEGMENTS), SEGMENTS)
        for sl in range(n_slab):
            xs_scr[sl, pl.ds(j, SEGMENTS, stride=pitch), :] = h_scr[sl, at, :] + h_scr[n_slab + sl, at, :]
        return carry

    lax.fori_loop(0, seg, scatter, 0, unroll=SCAN_UNROLL)

    for sg in range(SEGMENTS):
        for sl in range(n_slab):
            h = xs_scr[sl, sg * pitch:sg * pitch + seg, :]
            gy = gy_ref[0, sg * seg:(sg + 1) * seg, lanes(sl)].astype(F32)
            o_ref[0, sg * seg:(sg + 1) * seg, lanes(sl)] = (h * gy).astype(BF16)


def _lru_call(xl, gy, conv_w, conv_b, wg, bg, lam):
    b, s, _ = xl.shape
    ct = LRU_CH_TILE
    n_slab = ct // LRU_BLOCK
    pitch = s // SEGMENTS + SEG_PAD
    seq_blk = pl.BlockSpec((1, s, ct), lambda bi, c: (bi, 0, c))
    work = pltpu.VMEM((2 * n_slab, s, LRU_BLOCK), F32)
    return pl.pallas_call(
        functools.partial(_lru_kernel, seq=s),
        grid=(b, D_MODEL // ct),
        in_specs=[seq_blk, seq_blk,
                  pl.BlockSpec((CONV_WIDTH, ct), lambda bi, c: (0, c)),
                  pl.BlockSpec((1, ct), lambda bi, c: (0, c)),
                  pl.BlockSpec((n_slab, LRU_BLOCK, 4 * LRU_BLOCK), lambda bi, c: (c, 0, 0)),
                  pl.BlockSpec((n_slab, 1, 4 * LRU_BLOCK), lambda bi, c: (c, 0, 0)),
                  pl.BlockSpec((2, ct), lambda bi, c: (0, c))],
        out_specs=seq_blk,
        out_shape=jax.ShapeDtypeStruct((b, s, D_MODEL), BF16),
        scratch_shapes=[pltpu.VMEM((n_slab, SEGMENTS * pitch, LRU_BLOCK), F32), work, work, work],
        compiler_params=_params(2),
        name="lru",
    )(xl, gy, conv_w, conv_b, wg, bg, lam)


def _merge_kernel(x_ref, att_ref, hl_ref, sa_ref, sl_ref, mod_ref, gpre_ref, gpost_ref,
                  wao_ref, wlo_ref, wout_ref, wgu_ref, wd_ref, o_ref):
    o_att = _dot(att_ref[...], wao_ref[...])
    o_lru = _dot(hl_ref[...], wlo_ref[...])
    merged = sa_ref[...].astype(F32) * o_att + sl_ref[...].astype(F32) * o_lru
    m = _rms(_dot(merged.astype(BF16), wout_ref[...]), gpost_ref[1:2, :])
    x = x_ref[...] + mod_ref[0, 5:6, :] * m
    o_ref[...] = _ffn_half_step(x, mod_ref, 2, gpre_ref, gpost_ref, wgu_ref, wd_ref)


def _merge_call(x2d, att, hl, sa, sl, mod, g_pre, g_post, wao, wlo, wout, wgu, wd, seq):
    n = x2d.shape[0]
    tm = TOKEN_TILE
    per_seq = seq // tm
    tok = lambda w: pl.BlockSpec((tm, w), lambda i: (i, 0))
    return pl.pallas_call(
        _merge_kernel,
        grid=(n // tm,),
        in_specs=[tok(D_MODEL), tok(N_HEADS * V_DIM), tok(D_MODEL), tok(D_MODEL), tok(D_MODEL),
                  pl.BlockSpec((1, 9, D_MODEL), lambda i: (i // per_seq, 0, 0)),
                  _resident(g_pre.shape), _resident(g_post.shape), _resident(wao.shape),
                  _resident(wlo.shape), _resident(wout.shape), _resident(wgu.shape),
                  _resident(wd.shape)],
        out_specs=tok(D_MODEL),
        out_shape=jax.ShapeDtypeStruct((n, D_MODEL), F32),
        compiler_params=_params(1),
        name="merge",
    )(x2d, att, hl, sa, sl, mod, g_pre, g_post, wao, wlo, wout, wgu, wd)


def _rope_partner(w):
    half = QK_ROPE // 2
    return jnp.concatenate([-w[:, half:], w[:, :half]], axis=1)


def _pad_rope_cols(w):
    rows = w.shape[0]
    return jnp.concatenate([jnp.zeros((rows, QK_NOPE), w.dtype), w,
                            jnp.zeros((rows, HEAD_PAD - QK_DIM), w.dtype)], axis=1)


def _prep_layer(w_in, w_q_b, w_kv_b, lru_w_a, lru_b_a, lru_w_i, lru_b_i):
    q0, q1, q2, q3, q4, q5 = (Q_LORA, Q_LORA + KV_LORA, Q_LORA + KV_LORA + QK_ROPE,
                              Q_LORA + KV_LORA + QK_ROPE + D_MODEL,
                              Q_LORA + KV_LORA + QK_ROPE + 2 * D_MODEL,
                              Q_LORA + KV_LORA + QK_ROPE + 3 * D_MODEL)
    w_kr = w_in[:, q1:q2]
    win = jnp.concatenate([w_in[:, :q1], _pad_rope_cols(w_kr), _pad_rope_cols(_rope_partner(w_kr)),
                           w_in[:, q2:q3], w_in[:, q3:q4], w_in[:, q4:q5], w_in[:, q5:]],
                          axis=1).astype(BF16)

    wq = w_q_b.reshape(Q_LORA, N_HEADS, QK_DIM)
    pad = jnp.zeros((Q_LORA, N_HEADS, HEAD_PAD - QK_DIM), w_q_b.dtype)
    zero_nope = jnp.zeros((Q_LORA, N_HEADS, QK_NOPE), w_q_b.dtype)
    rope = wq[:, :, QK_NOPE:]
    half = QK_ROPE // 2
    partner = jnp.concatenate([-rope[:, :, half:], rope[:, :, :half]], axis=2)
    wq_main = jnp.concatenate([wq, pad], axis=2).reshape(Q_LORA, N_HEADS * HEAD_PAD)
    wq_part = jnp.concatenate([zero_nope, partner, pad], axis=2).reshape(Q_LORA, N_HEADS * HEAD_PAD)
    wq2 = jnp.concatenate([wq_main, wq_part], axis=1).astype(BF16)

    wkv = w_kv_b.reshape(KV_LORA, N_HEADS, QK_NOPE + V_DIM)
    wk = jnp.concatenate([wkv[:, :, :QK_NOPE],
                          jnp.zeros((KV_LORA, N_HEADS, HEAD_PAD - QK_NOPE), w_kv_b.dtype)],
                         axis=2).reshape(KV_LORA, N_HEADS * HEAD_PAD).astype(BF16)
    wv = wkv[:, :, QK_NOPE:].reshape(KV_LORA, N_HEADS * V_DIM).T.astype(BF16)

    wg = (0.5 * jnp.concatenate([lru_w_a[0], lru_w_i[0], lru_w_a[1], lru_w_i[1]], axis=2)).astype(BF16)
    nblk = D_MODEL // LRU_BLOCK
    bg = 0.5 * jnp.concatenate([lru_b_a[0].reshape(nblk, 1, LRU_BLOCK), lru_b_i[0].reshape(nblk, 1, LRU_BLOCK),
                                lru_b_a[1].reshape(nblk, 1, LRU_BLOCK), lru_b_i[1].reshape(nblk, 1, LRU_BLOCK)],
                               axis=2)
    return win, wq2, wk, wv, wg, bg


def _rope_tables(seq):
    inv = 1.0 / (ROPE_THETA ** (jnp.arange(0, QK_ROPE, 2, dtype=F32) / QK_ROPE))
    ang = jnp.arange(seq, dtype=F32)[:, None] * inv[None, :]
    cos, sin = jnp.cos(ang), jnp.sin(ang)
    ones = jnp.ones((seq, QK_NOPE), F32)
    tail1 = jnp.ones((seq, HEAD_PAD - QK_DIM), F32)
    cos_t = jnp.concatenate([ones, cos, cos, tail1], axis=1)
    sin_t = jnp.concatenate([0.0 * ones, sin, sin, 0.0 * tail1], axis=1)
    return cos_t, sin_t


def _encoder_layer(x, mod, lw):
    b, s, _ = x.shape
    x2d = x.reshape(b * s, D_MODEL)
    cos_t, sin_t = _rope_tables(s)
    x1 = _ffn_call(x2d, mod, lw["g_pre"], lw["g_post"], lw["wgu1"], lw["wd1"], s)
    q, k, vt, xl, gy, sa, sl = _inproj_call(x1, mod, lw["g_pre"], lw["win"], lw["gq"], lw["gkv"],
                                           lw["wq2"], lw["wk"], lw["wv"], cos_t, sin_t, s)
    att = _attn_call(q.reshape(b, s, -1), k.reshape(b, s, -1), vt)
    hl = _lru_call(xl.reshape(b, s, D_MODEL), gy.reshape(b, s, D_MODEL), lw["conv_w"], lw["conv_b"],
                   lw["wg"], lw["bg"], lw["lam"])
    y = _merge_call(x1, att.reshape(b * s, -1), hl.reshape(b * s, D_MODEL), sa, sl, mod,
                    lw["g_pre"], lw["g_post"], lw["wao"], lw["wlo"], lw["wout"], lw["wgu2"],
                    lw["wd2"], s)
    return y.reshape(b, s, D_MODEL)


def kernel(x_prompt, x_sample, c_prompt, c_sample, w_ada, b_ada, g_pre, g_post, w_ffn1_gu, w_ffn1_down, w_in, g_q_norm, g_kv_norm, w_q_b, w_kv_b, w_attn_o, conv_w, conv_b, lru_w_a, lru_b_a, lru_w_i, lru_b_i, lru_lambda, w_lru_o, w_out, w_ffn2_gu, w_ffn2_down):
    depth = w_ada.shape[0]
    n_prompt = c_prompt.shape[0]
    y_prompt, y_sample = x_prompt, x_sample
    for l in range(depth):
        win, wq2, wk, wv, wg, bg = _prep_layer(w_in[l], w_q_b[l], w_kv_b[l], lru_w_a[l], lru_b_a[l],
                                               lru_w_i[l], lru_b_i[l])
        lw = dict(g_pre=g_pre[l], g_post=g_post[l],
                  wgu1=w_ffn1_gu[l].astype(BF16), wd1=w_ffn1_down[l].astype(BF16),
                  win=win, gq=g_q_norm[l].reshape(1, Q_LORA), gkv=g_kv_norm[l].reshape(1, KV_LORA),
                  wq2=wq2, wk=wk, wv=wv, conv_w=conv_w[l], conv_b=conv_b[l].reshape(1, D_MODEL),
                  wg=wg, bg=bg, lam=lru_lambda[l],
                  wao=w_attn_o[l].astype(BF16), wlo=w_lru_o[l].astype(BF16),
                  wout=w_out[l].astype(BF16),
                  wgu2=w_ffn2_gu[l].astype(BF16), wd2=w_ffn2_down[l].astype(BF16))
        c_all = jnp.concatenate([c_prompt, c_sample], axis=0)
        mod = _modulation(c_all, w_ada[l], b_ada[l]).reshape(c_all.shape[0], 9, D_MODEL)
        y_prompt = _encoder_layer(y_prompt, mod[:n_prompt], lw)
        y_sample = _encoder_layer(y_sample, mod[n_prompt:], lw)
    return (y_prompt, y_sample)
```

```python
import functools
import math

import jax
import jax.numpy as jnp
from jax import lax
from jax.experimental import pallas as pl
from jax.experimental.pallas import tpu as pltpu

F32 = jnp.float32
BF16 = jnp.bfloat16

D_MODEL = 1024
N_HEADS = 8
QK_NOPE = 64
QK_ROPE = 32
QK_DIM = QK_NOPE + QK_ROPE
V_DIM = 64
Q_LORA = 256
KV_LORA = 128
HEAD_PAD = 128
ROPE_THETA = 10000.0
LRU_BLOCK = 128
CONV_WIDTH = 4
LRU_C = 8.0
D_FF = 2816
EPS = 1e-6
FFN_RESIDUAL = 0.5

FF_CHUNK = 256
TOKEN_TILE = 512
Q_TILE = 512
ONES_ROWS = 16
LRU_CH_TILE = 256
SEGMENTS = 8
SEG_PAD = 8
SCAN_UNROLL = 8
SCAN_PARTS = 2
LRU_STEP_CHUNK = 64
VMEM_LIMIT = 56 * 1024 * 1024

_C_Q = 0
_C_KV = _C_Q + Q_LORA
_C_KR = _C_KV + KV_LORA
_C_KRP = _C_KR + HEAD_PAD
_C_X = _C_KRP + HEAD_PAD
_C_Y = _C_X + D_MODEL
_C_GA = _C_Y + D_MODEL
_C_GL = _C_GA + D_MODEL
_C_END = _C_GL + D_MODEL


def _params(n_axes, flags=None):
    return pltpu.CompilerParams(dimension_semantics=("parallel",) * n_axes,
                                vmem_limit_bytes=VMEM_LIMIT, flags=flags)


def _resident(shape):
    nd = len(shape)
    return pl.BlockSpec(shape, lambda *_: (0,) * nd, pipeline_mode=pl.Buffered(1))


def _rms(x, g):
    y = x * lax.rsqrt(jnp.mean(x * x, axis=-1, keepdims=True) + EPS)
    return y * g


def _dot(a, b):
    return jnp.dot(a, b, preferred_element_type=F32)


def _dot_nt(a, b):
    return lax.dot_general(a, b, (((1,), (1,)), ((), ())), preferred_element_type=F32)


def _sigmoid(x):
    return 0.5 * (1.0 + jnp.tanh(0.5 * x))


def _mod_kernel(c_ref, w_ref, b_ref, o_ref):
    c = c_ref[...]
    s = (c * _sigmoid(c)).astype(BF16)
    o_ref[...] = _dot(s, w_ref[...].astype(BF16)) + b_ref[...]


def _modulation(c, w_ada, b_ada):
    nb = c.shape[0]
    n = w_ada.shape[1]
    tn = 1024
    return pl.pallas_call(
        _mod_kernel,
        grid=(n // tn,),
        in_specs=[pl.BlockSpec((nb, D_MODEL), lambda j: (0, 0)),
                  pl.BlockSpec((D_MODEL, tn), lambda j: (0, j)),
                  pl.BlockSpec((1, tn), lambda j: (0, j))],
        out_specs=pl.BlockSpec((nb, tn), lambda j: (0, j)),
        out_shape=jax.ShapeDtypeStruct((nb, n), F32),
        compiler_params=_params(1),
        name="mod",
    )(c, w_ada, b_ada.reshape(1, n))


def _swiglu(h, wgu_ref, wd_ref):
    acc = None
    for c in range(0, D_FF, FF_CHUNK):
        g = _dot(h, wgu_ref[:, c:c + FF_CHUNK])
        u = _dot(h, wgu_ref[:, D_FF + c:D_FF + c + FF_CHUNK])
        a = (g * _sigmoid(g) * u).astype(BF16)
        d = _dot(a, wd_ref[c:c + FF_CHUNK, :])
        acc = d if acc is None else acc + d
    return acc


def _ffn_half_step(x, mod_ref, sub, gpre_ref, gpost_ref, wgu_ref, wd_ref):
    shift = mod_ref[0, 3 * sub:3 * sub + 1, :]
    scl = mod_ref[0, 3 * sub + 1:3 * sub + 2, :]
    gate = mod_ref[0, 3 * sub + 2:3 * sub + 3, :]
    h = _rms(x, gpre_ref[sub:sub + 1, :]) * (1.0 + scl) + shift
    f = _rms(_swiglu(h.astype(BF16), wgu_ref, wd_ref), gpost_ref[sub:sub + 1, :])
    return x + (FFN_RESIDUAL * gate) * f


def _ffn_kernel(x_ref, mod_ref, gpre_ref, gpost_ref, wgu_ref, wd_ref, o_ref):
    o_ref[...] = _ffn_half_step(x_ref[...], mod_ref, 0, gpre_ref, gpost_ref, wgu_ref, wd_ref)


def _ffn_call(x2d, mod, g_pre, g_post, wgu, wd, seq):
    n = x2d.shape[0]
    tm = TOKEN_TILE
    per_seq = seq // tm
    tok = pl.BlockSpec((tm, D_MODEL), lambda i: (i, 0))
    return pl.pallas_call(
        _ffn_kernel,
        grid=(n // tm,),
        in_specs=[tok,
                  pl.BlockSpec((1, 9, D_MODEL), lambda i: (i // per_seq, 0, 0)),
                  _resident(g_pre.shape), _resident(g_post.shape),
                  _resident(wgu.shape), _resident(wd.shape)],
        out_specs=tok,
        out_shape=jax.ShapeDtypeStruct((n, D_MODEL), F32),
        compiler_params=_params(1),
        name="ffn",
    )(x2d, mod, g_pre, g_post, wgu, wd)


def _inproj_kernel(x_ref, mod_ref, gpre_ref, win_ref, gq_ref, gkv_ref, wq_ref, wk_ref, wv_ref,
                   cos_ref, sin_ref,
                   q_ref, k_ref, vt_ref, xl_ref, gy_ref, sa_ref, sl_ref):
    x = x_ref[...]
    shift = mod_ref[0, 3:4, :]
    scl = mod_ref[0, 4:5, :]
    h = (_rms(x, gpre_ref[1:2, :]) * (1.0 + scl) + shift).astype(BF16)
    cos = cos_ref[...]
    sin = sin_ref[...]

    lat = _dot(h, win_ref[:, _C_Q:_C_X])
    cq_n = _rms(lat[:, _C_Q:_C_KV], gq_ref[...]).astype(BF16)
    ckv_n = _rms(lat[:, _C_KV:_C_KR], gkv_ref[...]).astype(BF16)

    k_rope = lat[:, _C_KR:_C_KRP] * cos + lat[:, _C_KRP:_C_X] * sin
    scale = QK_DIM ** -0.5 * math.log2(math.e)
    n_q = N_HEADS * HEAD_PAD
    q_all = _dot(cq_n, wq_ref[...])
    k_all = _dot(ckv_n, wk_ref[...])
    for hd in range(N_HEADS):
        lo = hd * HEAD_PAD
        q_h = q_all[:, lo:lo + HEAD_PAD]
        q_p = q_all[:, n_q + lo:n_q + lo + HEAD_PAD]
        q_ref[:, lo:lo + HEAD_PAD] = ((q_h * cos + q_p * sin) * scale).astype(BF16)
        k_ref[:, lo:lo + HEAD_PAD] = (k_all[:, lo:lo + HEAD_PAD] + k_rope).astype(BF16)
    vt_ref[0] = _dot_nt(wv_ref[...], ckv_n).astype(BF16)

    xl_ref[...] = _dot(h, win_ref[:, _C_X:_C_Y]).astype(BF16)
    gy_ref[...] = jax.nn.gelu(_dot(h, win_ref[:, _C_Y:_C_GA])).astype(BF16)
    sa_ref[...] = _sigmoid(_dot(h, win_ref[:, _C_GA:_C_GL])).astype(BF16)
    sl_ref[...] = _sigmoid(_dot(h, win_ref[:, _C_GL:_C_END])).astype(BF16)


def _inproj_call(x2d, mod, g_pre, win, gq, gkv, wq, wk, wv, cos_t, sin_t, seq):
    n = x2d.shape[0]
    tm = TOKEN_TILE
    per_seq = seq // tm
    tok = lambda w: pl.BlockSpec((tm, w), lambda i: (i, 0))
    pos = pl.BlockSpec((tm, HEAD_PAD), lambda i: (i % per_seq, 0))
    wide = N_HEADS * HEAD_PAD
    n_v = N_HEADS * V_DIM
    rows = lambda w: (pl.BlockSpec((tm, w), lambda i: (i, 0)), jax.ShapeDtypeStruct((n, w), BF16))
    v_t = (pl.BlockSpec((1, n_v, tm), lambda i: (i // per_seq, 0, i % per_seq)),
           jax.ShapeDtypeStruct((n // seq, n_v, seq), BF16))
    outs = [rows(wide), rows(wide), v_t] + [rows(D_MODEL)] * 4
    return pl.pallas_call(
        _inproj_kernel,
        grid=(n // tm,),
        in_specs=[tok(D_MODEL),
                  pl.BlockSpec((1, 9, D_MODEL), lambda i: (i // per_seq, 0, 0)),
                  _resident(g_pre.shape), _resident(win.shape), _resident(gq.shape),
                  _resident(gkv.shape), _resident(wq.shape), _resident(wk.shape),
                  _resident(wv.shape), pos, pos],
        out_specs=[spec for spec, _ in outs],
        out_shape=[shape for _, shape in outs],
        compiler_params=_params(1),
        name="inproj",
    )(x2d, mod, g_pre, win, gq, gkv, wq, wk, wv, cos_t, sin_t)


def _attn_kernel(q_ref, k_ref, vt_ref, o_ref):
    def scores(hd):
        lo = hd * HEAD_PAD
        q = q_ref[0, :, lo:lo + HEAD_PAD]
        k = k_ref[0, :, lo:lo + HEAD_PAD]
        return _dot_nt(k, q)

    ones = jnp.ones((ONES_ROWS, k_ref.shape[1]), BF16)
    halves = []
    st_next = scores(0)
    for hd in range(N_HEADS):
        st = st_next
        if hd + 1 < N_HEADS:
            st_next = scores(hd + 1)
        m = jnp.max(st, axis=0, keepdims=True)
        p = jnp.exp2(st - m).astype(BF16)
        vt = jnp.concatenate([vt_ref[0, hd * V_DIM:(hd + 1) * V_DIM, :], ones], axis=0)
        o = _dot(vt, p)
        halves.append(o[:V_DIM] / o[V_DIM:V_DIM + 1])
        if hd % 2 == 1:
            pair = hd // 2
            o_pair = jnp.concatenate(halves, axis=0).T
            o_ref[0, :, pair * 2 * V_DIM:(pair + 1) * 2 * V_DIM] = o_pair.astype(BF16)
            halves = []


def _attn_call(q, k, vt):
    b, s, _ = q.shape
    tq = Q_TILE
    return pl.pallas_call(
        _attn_kernel,
        grid=(b, s // tq),
        in_specs=[pl.BlockSpec((1, tq, N_HEADS * HEAD_PAD), lambda bi, i: (bi, i, 0)),
                  pl.BlockSpec((1, s, N_HEADS * HEAD_PAD), lambda bi, i: (bi, 0, 0),
                               pipeline_mode=pl.Buffered(1)),
                  pl.BlockSpec((1, N_HEADS * V_DIM, s), lambda bi, i: (bi, 0, 0),
                               pipeline_mode=pl.Buffered(1))],
        out_specs=pl.BlockSpec((1, tq, N_HEADS * V_DIM), lambda bi, i: (bi, i, 0)),
        out_shape=jax.ShapeDtypeStruct((b, s, N_HEADS * V_DIM), BF16),
        compiler_params=_params(2),
        name="attn",
    )(q, k, vt)


def _lru_kernel(x_ref, gy_ref, cw_ref, cb_ref, wg_ref, bg_ref, lam_ref, o_ref,
                xs_scr, a_scr, u_scr, h_scr, *, seq):
    n_slab = LRU_CH_TILE // LRU_BLOCK
    n_ch = 2 * n_slab
    seg = seq // SEGMENTS
    pitch = seg + SEG_PAD
    tile = (SEGMENTS, LRU_BLOCK)

    def lanes(sl):
        return slice(sl * LRU_BLOCK, (sl + 1) * LRU_BLOCK)

    for sg in range(SEGMENTS):
        for sl in range(n_slab):
            xs_scr[sl, sg * pitch:sg * pitch + seg, :] = (
                x_ref[0, sg * seg:(sg + 1) * seg, lanes(sl)].astype(F32))

    lam = lam_ref[...]
    neg_lam = -lam
    softplus = jnp.maximum(neg_lam, 0.0) + jnp.log1p(jnp.exp(-jnp.abs(neg_lam)))
    decay = -LRU_C * softplus

    row = lax.broadcasted_iota(jnp.int32, tile, 0)

    def step_tile(sl, j):
        if j < 0:
            return jnp.where(row == 0, 0.0, pltpu.roll(step_tile(sl, seg + j), 1, axis=0))
        if j >= seg:
            return jnp.where(row == SEGMENTS - 1, 0.0,
                             pltpu.roll(step_tile(sl, j - seg), SEGMENTS - 1, axis=0))
        return xs_scr[sl, pl.ds(j, SEGMENTS, stride=pitch), :]

    left = (CONV_WIDTH - 1) // 2
    for c0 in range(0, seg, LRU_STEP_CHUNK):
        for sl in range(n_slab):
            tiles = {j: step_tile(sl, j)
                     for j in range(c0 - left, c0 + LRU_STEP_CHUNK + CONV_WIDTH - 1 - left)}
            xc = cb_ref[:, lanes(sl)]
            for t in range(CONV_WIDTH):
                tap = jnp.concatenate([tiles[j + t - left] for j in range(c0, c0 + LRU_STEP_CHUNK)],
                                      axis=0)
                xc = xc + tap * cw_ref[t:t + 1, lanes(sl)]
            g = _dot(xc.astype(BF16), wg_ref[sl]) + bg_ref[sl]
            half_x = 0.5 * xc
            span = slice(c0 * SEGMENTS, (c0 + LRU_STEP_CHUNK) * SEGMENTS)
            for d in range(2):
                t_r = jnp.tanh(g[:, (2 * d) * LRU_BLOCK:(2 * d + 1) * LRU_BLOCK])
                t_i = jnp.tanh(g[:, (2 * d + 1) * LRU_BLOCK:(2 * d + 2) * LRU_BLOCK])
                half_decay = 0.5 * decay[d:d + 1, lanes(sl)]
                log_a = half_decay + half_decay * t_r
                a = jnp.exp(log_a)
                w = -jnp.tanh(log_a) * (1.0 + a * a)
                root = jnp.where(w > 0.0, w * lax.rsqrt(w), 0.0)
                a_scr[d * n_slab + sl, span, :] = a
                u_scr[d * n_slab + sl, span, :] = root * (half_x + half_x * t_i)

    sub = seg // SCAN_PARTS
    chains = [(ch, part) for ch in range(n_ch) for part in range(SCAN_PARTS)]

    def rows(ch, part, j):
        idx = part * sub + j if ch < n_slab else (part + 1) * sub - 1 - j
        return pl.ds(pl.multiple_of(idx * SEGMENTS, SEGMENTS), SEGMENTS)

    def summarize(j, carry):
        out = []
        for (ch, part), (h, p) in zip(chains, carry):
            a = a_scr[ch, rows(ch, part, j), :]
            u = u_scr[ch, rows(ch, part, j), :]
            out.append((a * h + u, p * a))
        return tuple(out)

    init = tuple((jnp.zeros(tile, F32), jnp.ones(tile, F32)) for _ in chains)
    ends = dict(zip(chains, lax.fori_loop(0, sub, summarize, init, unroll=SCAN_UNROLL)))

    starts = {}
    for ch in range(n_ch):
        forward = ch < n_slab
        pieces = [(sg, part) for sg in range(SEGMENTS) for part in range(SCAN_PARTS)]
        c = jnp.zeros((1, LRU_BLOCK), F32)
        entering = {}
        for sg, part in (pieces if forward else reversed(pieces)):
            entering[sg, part] = c
            h_end, p_end = ends[ch, part]
            c = h_end[sg:sg + 1, :] + p_end[sg:sg + 1, :] * c
        for part in range(SCAN_PARTS):
            starts[ch, part] = jnp.concatenate([entering[sg, part] for sg in range(SEGMENTS)], axis=0)

    def replay(j, carry):
        out = []
        for (ch, part), h_prev in zip(chains, carry):
            at = rows(ch, part, j)
            h = a_scr[ch, at, :] * h_prev + u_scr[ch, at, :]
            h_scr[ch, at, :] = h
            out.append(h)
        return tuple(out)

    lax.fori_loop(0, sub, replay, tuple(starts[c] for c in chains), unroll=SCAN_UNROLL)

    def scatter(j, carry):
        at = pl.ds(pl.multiple_of(j * SEGMENTS, SEGMENTS), SEGMENTS)
        for sl in range(n_slab):
            xs_scr[sl, pl.ds(j, SEGMENTS, stride=pitch), :] = h_scr[sl, at, :] + h_scr[n_slab + sl, at, :]
        return carry

    lax.fori_loop(0, seg, scatter, 0, unroll=SCAN_UNROLL)

    for sg in range(SEGMENTS):
        for sl in range(n_slab):
            h = xs_scr[sl, sg * pitch:sg * pitch + seg, :]
            gy = gy_ref[0, sg * seg:(sg + 1) * seg, lanes(sl)].astype(F32)
            o_ref[0, sg * seg:(sg + 1) * seg, lanes(sl)] = (h * gy).astype(BF16)


def _lru_call(xl, gy, conv_w, conv_b, wg, bg, lam):
    b, s, _ = xl.shape
    ct = LRU_CH_TILE
    n_slab = ct // LRU_BLOCK
    pitch = s // SEGMENTS + SEG_PAD
    seq_blk = pl.BlockSpec((1, s, ct), lambda bi, c: (bi, 0, c))
    work = pltpu.VMEM((2 * n_slab, s, LRU_BLOCK), F32)
    return pl.pallas_call(
        functools.partial(_lru_kernel, seq=s),
        grid=(b, D_MODEL // ct),
        in_specs=[seq_blk, seq_blk,
                  pl.BlockSpec((CONV_WIDTH, ct), lambda bi, c: (0, c)),
                  pl.BlockSpec((1, ct), lambda bi, c: (0, c)),
                  pl.BlockSpec((n_slab, LRU_BLOCK, 4 * LRU_BLOCK), lambda bi, c: (c, 0, 0)),
                  pl.BlockSpec((n_slab, 1, 4 * LRU_BLOCK), lambda bi, c: (c, 0, 0)),
                  pl.BlockSpec((2, ct), lambda bi, c: (0, c))],
        out_specs=seq_blk,
        out_shape=jax.ShapeDtypeStruct((b, s, D_MODEL), BF16),
        scratch_shapes=[pltpu.VMEM((n_slab, SEGMENTS * pitch, LRU_BLOCK), F32), work, work, work],
        compiler_params=_params(2),
        name="lru",
    )(xl, gy, conv_w, conv_b, wg, bg, lam)


def _merge_kernel(x_ref, att_ref, hl_ref, sa_ref, sl_ref, mod_ref, gpre_ref, gpost_ref,
                  wao_ref, wlo_ref, wout_ref, wgu_ref, wd_ref, o_ref):
    o_att = _dot(att_ref[...], wao_ref[...])
    o_lru = _dot(hl_ref[...], wlo_ref[...])
    merged = sa_ref[...].astype(F32) * o_att + sl_ref[...].astype(F32) * o_lru
    m = _rms(_dot(merged.astype(BF16), wout_ref[...]), gpost_ref[1:2, :])
    x = x_ref[...] + mod_ref[0, 5:6, :] * m
    o_ref[...] = _ffn_half_step(x, mod_ref, 2, gpre_ref, gpost_ref, wgu_ref, wd_ref)


def _merge_call(x2d, att, hl, sa, sl, mod, g_pre, g_post, wao, wlo, wout, wgu, wd, seq):
    n = x2d.shape[0]
    tm = TOKEN_TILE
    per_seq = seq // tm
    tok = lambda w: pl.BlockSpec((tm, w), lambda i: (i, 0))
    return pl.pallas_call(
        _merge_kernel,
        grid=(n // tm,),
        in_specs=[tok(D_MODEL), tok(N_HEADS * V_DIM), tok(D_MODEL), tok(D_MODEL), tok(D_MODEL),
                  pl.BlockSpec((1, 9, D_MODEL), lambda i: (i // per_seq, 0, 0)),
                  _resident(g_pre.shape), _resident(g_post.shape), _resident(wao.shape),
                  _resident(wlo.shape), _resident(wout.shape), _resident(wgu.shape),
                  _resident(wd.shape)],
        out_specs=tok(D_MODEL),
        out_shape=jax.ShapeDtypeStruct((n, D_MODEL), F32),
        compiler_params=_params(1),
        name="merge",
    )(x2d, att, hl, sa, sl, mod, g_pre, g_post, wao, wlo, wout, wgu, wd)


def _rope_partner(w):
    half = QK_ROPE // 2
    return jnp.concatenate([-w[:, half:], w[:, :half]], axis=1)


def _pad_rope_cols(w):
    rows = w.shape[0]
    return jnp.concatenate([jnp.zeros((rows, QK_NOPE), w.dtype), w,
                            jnp.zeros((rows, HEAD_PAD - QK_DIM), w.dtype)], axis=1)


def _prep_layer(w_in, w_q_b, w_kv_b, lru_w_a, lru_b_a, lru_w_i, lru_b_i):
    q0, q1, q2, q3, q4, q5 = (Q_LORA, Q_LORA + KV_LORA, Q_LORA + KV_LORA + QK_ROPE,
                              Q_LORA + KV_LORA + QK_ROPE + D_MODEL,
                              Q_LORA + KV_LORA + QK_ROPE + 2 * D_MODEL,
                              Q_LORA + KV_LORA + QK_ROPE + 3 * D_MODEL)
    w_kr = w_in[:, q1:q2]
    win = jnp.concatenate([w_in[:, :q1], _pad_rope_cols(w_kr), _pad_rope_cols(_rope_partner(w_kr)),
                           w_in[:, q2:q3], w_in[:, q3:q4], w_in[:, q4:q5], w_in[:, q5:]],
                          axis=1).astype(BF16)

    wq = w_q_b.reshape(Q_LORA, N_HEADS, QK_DIM)
    pad = jnp.zeros((Q_LORA, N_HEADS, HEAD_PAD - QK_DIM), w_q_b.dtype)
    zero_nope = jnp.zeros((Q_LORA, N_HEADS, QK_NOPE), w_q_b.dtype)
    rope = wq[:, :, QK_NOPE:]
    half = QK_ROPE // 2
    partner = jnp.concatenate([-rope[:, :, half:], rope[:, :, :half]], axis=2)
    wq_main = jnp.concatenate([wq, pad], axis=2).reshape(Q_LORA, N_HEADS * HEAD_PAD)
    wq_part = jnp.concatenate([zero_nope, partner, pad], axis=2).reshape(Q_LORA, N_HEADS * HEAD_PAD)
    wq2 = jnp.concatenate([wq_main, wq_part], axis=1).astype(BF16)

    wkv = w_kv_b.reshape(KV_LORA, N_HEADS, QK_NOPE + V_DIM)
    wk = jnp.concatenate([wkv[:, :, :QK_NOPE],
                          jnp.zeros((KV_LORA, N_HEADS, HEAD_PAD - QK_NOPE), w_kv_b.dtype)],
                         axis=2).reshape(KV_LORA, N_HEADS * HEAD_PAD).astype(BF16)
    wv = wkv[:, :, QK_NOPE:].reshape(KV_LORA, N_HEADS * V_DIM).T.astype(BF16)

    wg = (0.5 * jnp.concatenate([lru_w_a[0], lru_w_i[0], lru_w_a[1], lru_w_i[1]], axis=2)).astype(BF16)
    nblk = D_MODEL // LRU_BLOCK
    bg = 0.5 * jnp.concatenate([lru_b_a[0].reshape(nblk, 1, LRU_BLOCK), lru_b_i[0].reshape(nblk, 1, LRU_BLOCK),
                                lru_b_a[1].reshape(nblk, 1, LRU_BLOCK), lru_b_i[1].reshape(nblk, 1, LRU_BLOCK)],
                               axis=2)
    return win, wq2, wk, wv, wg, bg


def _rope_tables(seq):
    inv = 1.0 / (ROPE_THETA ** (jnp.arange(0, QK_ROPE, 2, dtype=F32) / QK_ROPE))
    ang = jnp.arange(seq, dtype=F32)[:, None] * inv[None, :]
    cos, sin = jnp.cos(ang), jnp.sin(ang)
    ones = jnp.ones((seq, QK_NOPE), F32)
    tail1 = jnp.ones((seq, HEAD_PAD - QK_DIM), F32)
    cos_t = jnp.concatenate([ones, cos, cos, tail1], axis=1)
    sin_t = jnp.concatenate([0.0 * ones, sin, sin, 0.0 * tail1], axis=1)
    return cos_t, sin_t


def _encoder_layer(x, mod, lw):
    b, s, _ = x.shape
    x2d = x.reshape(b * s, D_MODEL)
    cos_t, sin_t = _rope_tables(s)
    x1 = _ffn_call(x2d, mod, lw["g_pre"], lw["g_post"], lw["wgu1"], lw["wd1"], s)
    q, k, vt, xl, gy, sa, sl = _inproj_call(x1, mod, lw["g_pre"], lw["win"], lw["gq"], lw["gkv"],
                                           lw["wq2"], lw["wk"], lw["wv"], cos_t, sin_t, s)
    att = _attn_call(q.reshape(b, s, -1), k.reshape(b, s, -1), vt)
    hl = _lru_call(xl.reshape(b, s, D_MODEL), gy.reshape(b, s, D_MODEL), lw["conv_w"], lw["conv_b"],
                   lw["wg"], lw["bg"], lw["lam"])
    y = _merge_call(x1, att.reshape(b * s, -1), hl.reshape(b * s, D_MODEL), sa, sl, mod,
                    lw["g_pre"], lw["g_post"], lw["wao"], lw["wlo"], lw["wout"], lw["wgu2"],
                    lw["wd2"], s)
    return y.reshape(b, s, D_MODEL)


def kernel(x_prompt, x_sample, c_prompt, c_sample, w_ada, b_ada, g_pre, g_post, w_ffn1_gu, w_ffn1_down, w_in, g_q_norm, g_kv_norm, w_q_b, w_kv_b, w_attn_o, conv_w, conv_b, lru_w_a, lru_b_a, lru_w_i, lru_b_i, lru_lambda, w_lru_o, w_out, w_ffn2_gu, w_ffn2_down):
    depth = w_ada.shape[0]
    n_prompt = c_prompt.shape[0]
    y_prompt, y_sample = x_prompt, x_sample
    for l in range(depth):
        win, wq2, wk, wv, wg, bg = _prep_layer(w_in[l], w_q_b[l], w_kv_b[l], lru_w_a[l], lru_b_a[l],
                                               lru_w_i[l], lru_b_i[l])
        lw = dict(g_pre=g_pre[l], g_post=g_post[l],
                  wgu1=w_ffn1_gu[l].astype(BF16), wd1=w_ffn1_down[l].astype(BF16),
                  win=win, gq=g_q_norm[l].reshape(1, Q_LORA), gkv=g_kv_norm[l].reshape(1, KV_LORA),
                  wq2=wq2, wk=wk, wv=wv, conv_w=conv_w[l], conv_b=conv_b[l].reshape(1, D_MODEL),
                  wg=wg, bg=bg, lam=lru_lambda[l],
                  wao=w_attn_o[l].astype(BF16), wlo=w_lru_o[l].astype(BF16),
                  wout=w_out[l].astype(BF16),
                  wgu2=w_ffn2_gu[l].astype(BF16), wd2=w_ffn2_down[l].astype(BF16))
        c_all = jnp.concatenate([c_prompt, c_sample], axis=0)
        mod = _modulation(c_all, w_ada[l], b_ada[l]).reshape(c_all.shape[0], 9, D_MODEL)
        y_prompt = _encoder_layer(y_prompt, mod[:n_prompt], lw)
        y_sample = _encoder_layer(y_sample, mod[n_prompt:], lw)
    return (y_prompt, y_sample)
```

```python
import functools
import math

import jax
import jax.numpy as jnp
from jax import lax
from jax.experimental import pallas as pl
from jax.experimental.pallas import tpu as pltpu

F32 = jnp.float32
BF16 = jnp.bfloat16

D_MODEL = 1024
N_HEADS = 8
QK_NOPE = 64
QK_ROPE = 32
QK_DIM = QK_NOPE + QK_ROPE
V_DIM = 64
Q_LORA = 256
KV_LORA = 128
HEAD_PAD = 128
ROPE_THETA = 10000.0
LRU_BLOCK = 128
CONV_WIDTH = 4
LRU_C = 8.0
D_FF = 2816
EPS = 1e-6
FFN_RESIDUAL = 0.5

FF_CHUNK = 256
TOKEN_TILE = 512
Q_TILE = 1024
Q_SUB = 512
ONES_ROWS = 16
LRU_CH_TILE = 256
SEGMENTS = 8
SEG_PAD = 8
SCAN_UNROLL = 8
SCAN_PARTS = 2
LRU_STEP_CHUNK = 64
VMEM_LIMIT = 56 * 1024 * 1024

_C_Q = 0
_C_KV = _C_Q + Q_LORA
_C_KR = _C_KV + KV_LORA
_C_KRP = _C_KR + HEAD_PAD
_C_X = _C_KRP + HEAD_PAD
_C_Y = _C_X + D_MODEL
_C_GA = _C_Y + D_MODEL
_C_GL = _C_GA + D_MODEL
_C_END = _C_GL + D_MODEL


def _params(n_axes, flags=None):
    return pltpu.CompilerParams(dimension_semantics=("parallel",) * n_axes,
                                vmem_limit_bytes=VMEM_LIMIT, flags=flags)


def _resident(shape):
    nd = len(shape)
    return pl.BlockSpec(shape, lambda *_: (0,) * nd, pipeline_mode=pl.Buffered(1))


def _rms(x, g):
    y = x * lax.rsqrt(jnp.mean(x * x, axis=-1, keepdims=True) + EPS)
    return y * g


def _dot(a, b):
    return jnp.dot(a, b, preferred_element_type=F32)


def _dot_nt(a, b):
    return lax.dot_general(a, b, (((1,), (1,)), ((), ())), preferred_element_type=F32)


def _sigmoid(x):
    return 0.5 * (1.0 + jnp.tanh(0.5 * x))


def _mod_kernel(c_ref, w_ref, b_ref, o_ref):
    c = c_ref[...]
    s = (c * _sigmoid(c)).astype(BF16)
    o_ref[...] = _dot(s, w_ref[...].astype(BF16)) + b_ref[...]


def _modulation(c, w_ada, b_ada):
    nb = c.shape[0]
    n = w_ada.shape[1]
    tn = 1024
    return pl.pallas_call(
        _mod_kernel,
        grid=(n // tn,),
        in_specs=[pl.BlockSpec((nb, D_MODEL), lambda j: (0, 0)),
                  pl.BlockSpec((D_MODEL, tn), lambda j: (0, j)),
                  pl.BlockSpec((1, tn), lambda j: (0, j))],
        out_specs=pl.BlockSpec((nb, tn), lambda j: (0, j)),
        out_shape=jax.ShapeDtypeStruct((nb, n), F32),
        compiler_params=_params(1),
        name="mod",
    )(c, w_ada, b_ada.reshape(1, n))


def _swiglu(h, wgu_ref, wd_ref):
    acc = None
    for c in range(0, D_FF, FF_CHUNK):
        g = _dot(h, wgu_ref[:, c:c + FF_CHUNK])
        u = _dot(h, wgu_ref[:, D_FF + c:D_FF + c + FF_CHUNK])
        a = (g * _sigmoid(g) * u).astype(BF16)
        d = _dot(a, wd_ref[c:c + FF_CHUNK, :])
        acc = d if acc is None else acc + d
    return acc


def _ffn_half_step(x, mod_ref, sub, gpre_ref, gpost_ref, wgu_ref, wd_ref):
    shift = mod_ref[0, 3 * sub:3 * sub + 1, :]
    scl = mod_ref[0, 3 * sub + 1:3 * sub + 2, :]
    gate = mod_ref[0, 3 * sub + 2:3 * sub + 3, :]
    h = _rms(x, gpre_ref[sub:sub + 1, :]) * (1.0 + scl) + shift
    f = _rms(_swiglu(h.astype(BF16), wgu_ref, wd_ref), gpost_ref[sub:sub + 1, :])
    return x + (FFN_RESIDUAL * gate) * f


def _ffn_kernel(x_ref, mod_ref, gpre_ref, gpost_ref, wgu_ref, wd_ref, o_ref):
    o_ref[...] = _ffn_half_step(x_ref[...], mod_ref, 0, gpre_ref, gpost_ref, wgu_ref, wd_ref)


def _ffn_call(x2d, mod, g_pre, g_post, wgu, wd, seq):
    n = x2d.shape[0]
    tm = TOKEN_TILE
    per_seq = seq // tm
    tok = pl.BlockSpec((tm, D_MODEL), lambda i: (i, 0))
    return pl.pallas_call(
        _ffn_kernel,
        grid=(n // tm,),
        in_specs=[tok,
                  pl.BlockSpec((1, 9, D_MODEL), lambda i: (i // per_seq, 0, 0)),
                  _resident(g_pre.shape), _resident(g_post.shape),
                  _resident(wgu.shape), _resident(wd.shape)],
        out_specs=tok,
        out_shape=jax.ShapeDtypeStruct((n, D_MODEL), F32),
        compiler_params=_params(1),
        name="ffn",
    )(x2d, mod, g_pre, g_post, wgu, wd)


def _inproj_kernel(x_ref, mod_ref, gpre_ref, win_ref, gq_ref, gkv_ref, wq_ref, wk_ref, wv_ref,
                   cos_ref, sin_ref,
                   q_ref, k_ref, vt_ref, xl_ref, gy_ref, sa_ref, sl_ref):
    x = x_ref[...]
    shift = mod_ref[0, 3:4, :]
    scl = mod_ref[0, 4:5, :]
    h = (_rms(x, gpre_ref[1:2, :]) * (1.0 + scl) + shift).astype(BF16)
    cos = cos_ref[...]
    sin = sin_ref[...]

    lat = _dot(h, win_ref[:, _C_Q:_C_X])
    cq_n = _rms(lat[:, _C_Q:_C_KV], gq_ref[...]).astype(BF16)
    ckv_n = _rms(lat[:, _C_KV:_C_KR], gkv_ref[...]).astype(BF16)

    k_rope = lat[:, _C_KR:_C_KRP] * cos + lat[:, _C_KRP:_C_X] * sin
    scale = QK_DIM ** -0.5 * math.log2(math.e)
    n_q = N_HEADS * HEAD_PAD
    q_all = _dot(cq_n, wq_ref[...])
    k_all = _dot(ckv_n, wk_ref[...])
    for hd in range(N_HEADS):
        lo = hd * HEAD_PAD
        q_h = q_all[:, lo:lo + HEAD_PAD]
        q_p = q_all[:, n_q + lo:n_q + lo + HEAD_PAD]
        q_ref[:, lo:lo + HEAD_PAD] = ((q_h * cos + q_p * sin) * scale).astype(BF16)
        k_ref[:, lo:lo + HEAD_PAD] = (k_all[:, lo:lo + HEAD_PAD] + k_rope).astype(BF16)
    vt_ref[0] = _dot_nt(wv_ref[...], ckv_n).astype(BF16)

    xl_ref[...] = _dot(h, win_ref[:, _C_X:_C_Y]).astype(BF16)
    gy_ref[...] = jax.nn.gelu(_dot(h, win_ref[:, _C_Y:_C_GA])).astype(BF16)
    sa_ref[...] = _sigmoid(_dot(h, win_ref[:, _C_GA:_C_GL])).astype(BF16)
    sl_ref[...] = _sigmoid(_dot(h, win_ref[:, _C_GL:_C_END])).astype(BF16)


def _inproj_call(x2d, mod, g_pre, win, gq, gkv, wq, wk, wv, cos_t, sin_t, seq):
    n = x2d.shape[0]
    tm = TOKEN_TILE
    per_seq = seq // tm
    tok = lambda w: pl.BlockSpec((tm, w), lambda i: (i, 0))
    pos = pl.BlockSpec((tm, HEAD_PAD), lambda i: (i % per_seq, 0))
    wide = N_HEADS * HEAD_PAD
    n_v = N_HEADS * V_DIM
    rows = lambda w: (pl.BlockSpec((tm, w), lambda i: (i, 0)), jax.ShapeDtypeStruct((n, w), BF16))
    v_t = (pl.BlockSpec((1, n_v, tm), lambda i: (i // per_seq, 0, i % per_seq)),
           jax.ShapeDtypeStruct((n // seq, n_v, seq), BF16))
    outs = [rows(wide), rows(wide), v_t] + [rows(D_MODEL)] * 4
    return pl.pallas_call(
        _inproj_kernel,
        grid=(n // tm,),
        in_specs=[tok(D_MODEL),
                  pl.BlockSpec((1, 9, D_MODEL), lambda i: (i // per_seq, 0, 0)),
                  _resident(g_pre.shape), _resident(win.shape), _resident(gq.shape),
                  _resident(gkv.shape), _resident(wq.shape), _resident(wk.shape),
                  _resident(wv.shape), pos, pos],
        out_specs=[spec for spec, _ in outs],
        out_shape=[shape for _, shape in outs],
        compiler_params=_params(1),
        name="inproj",
    )(x2d, mod, g_pre, win, gq, gkv, wq, wk, wv, cos_t, sin_t)


def _attn_kernel(q_ref, k_ref, vt_ref, o_ref):
    units = [(qt, hd) for qt in range(q_ref.shape[1] // Q_SUB) for hd in range(N_HEADS)]

    def scores(unit):
        qt, hd = unit
        lo = hd * HEAD_PAD
        q = q_ref[0, qt * Q_SUB:(qt + 1) * Q_SUB, lo:lo + HEAD_PAD]
        k = k_ref[0, :, lo:lo + HEAD_PAD]
        return _dot_nt(k, q)

    ones = jnp.ones((ONES_ROWS, k_ref.shape[1]), BF16)
    halves = []
    st_next = scores(units[0])
    for idx, (qt, hd) in enumerate(units):
        st = st_next
        if idx + 1 < len(units):
            st_next = scores(units[idx + 1])
        m = jnp.max(st, axis=0, keepdims=True)
        p = jnp.exp2(st - m).astype(BF16)
        vt = jnp.concatenate([vt_ref[0, hd * V_DIM:(hd + 1) * V_DIM, :], ones], axis=0)
        o = _dot(vt, p)
        halves.append(o[:V_DIM] / o[V_DIM:V_DIM + 1])
        if hd % 2 == 1:
            pair = hd // 2
            o_pair = jnp.concatenate(halves, axis=0).T
            o_ref[0, qt * Q_SUB:(qt + 1) * Q_SUB, pair * 2 * V_DIM:(pair + 1) * 2 * V_DIM] = (
                o_pair.astype(BF16))
            halves = []


def _attn_call(q, k, vt):
    b, s, _ = q.shape
    tq = Q_TILE
    return pl.pallas_call(
        _attn_kernel,
        grid=(b, s // tq),
        in_specs=[pl.BlockSpec((1, tq, N_HEADS * HEAD_PAD), lambda bi, i: (bi, i, 0)),
                  pl.BlockSpec((1, s, N_HEADS * HEAD_PAD), lambda bi, i: (bi, 0, 0)),
                  pl.BlockSpec((1, N_HEADS * V_DIM, s), lambda bi, i: (bi, 0, 0))],
        out_specs=pl.BlockSpec((1, tq, N_HEADS * V_DIM), lambda bi, i: (bi, i, 0)),
        out_shape=jax.ShapeDtypeStruct((b, s, N_HEADS * V_DIM), BF16),
        compiler_params=_params(2),
        name="attn",
    )(q, k, vt)


def _lru_kernel(x_ref, gy_ref, cw_ref, cb_ref, wg_ref, bg_ref, lam_ref, o_ref,
                xs_scr, a_scr, u_scr, h_scr, *, seq):
    n_slab = LRU_CH_TILE // LRU_BLOCK
    n_ch = 2 * n_slab
    seg = seq // SEGMENTS
    pitch = seg + SEG_PAD
    tile = (SEGMENTS, LRU_BLOCK)

    def lanes(sl):
        return slice(sl * LRU_BLOCK, (sl + 1) * LRU_BLOCK)

    for sg in range(SEGMENTS):
        for sl in range(n_slab):
            xs_scr[sl, sg * pitch:sg * pitch + seg, :] = (
                x_ref[0, sg * seg:(sg + 1) * seg, lanes(sl)].astype(F32))

    lam = lam_ref[...]
    neg_lam = -lam
    softplus = jnp.maximum(neg_lam, 0.0) + jnp.log1p(jnp.exp(-jnp.abs(neg_lam)))
    decay = -LRU_C * softplus

    row = lax.broadcasted_iota(jnp.int32, tile, 0)

    def step_tile(sl, j):
        if j < 0:
            return jnp.where(row == 0, 0.0, pltpu.roll(step_tile(sl, seg + j), 1, axis=0))
        if j >= seg:
            return jnp.where(row == SEGMENTS - 1, 0.0,
                             pltpu.roll(step_tile(sl, j - seg), SEGMENTS - 1, axis=0))
        return xs_scr[sl, pl.ds(j, SEGMENTS, stride=pitch), :]

    left = (CONV_WIDTH - 1) // 2
    for c0 in range(0, seg, LRU_STEP_CHUNK):
        for sl in range(n_slab):
            tiles = {j: step_tile(sl, j)
                     for j in range(c0 - left, c0 + LRU_STEP_CHUNK + CONV_WIDTH - 1 - left)}
            xc = cb_ref[:, lanes(sl)]
            for t in range(CONV_WIDTH):
                tap = jnp.concatenate([tiles[j + t - left] for j in range(c0, c0 + LRU_STEP_CHUNK)],
                                      axis=0)
                xc = xc + tap * cw_ref[t:t + 1, lanes(sl)]
            g = _dot(xc.astype(BF16), wg_ref[sl]) + bg_ref[sl]
            half_x = 0.5 * xc
            span = slice(c0 * SEGMENTS, (c0 + LRU_STEP_CHUNK) * SEGMENTS)
            for d in range(2):
                t_r = jnp.tanh(g[:, (2 * d) * LRU_BLOCK:(2 * d + 1) * LRU_BLOCK])
                t_i = jnp.tanh(g[:, (2 * d + 1) * LRU_BLOCK:(2 * d + 2) * LRU_BLOCK])
                half_decay = 0.5 * decay[d:d + 1, lanes(sl)]
                log_a = half_decay + half_decay * t_r
                a = jnp.exp(log_a)
                w = -jnp.tanh(log_a) * (1.0 + a * a)
                root = jnp.where(w > 0.0, w * lax.rsqrt(w), 0.0)
                a_scr[d * n_slab + sl, span, :] = a
                u_scr[d * n_slab + sl, span, :] = root * (half_x + half_x * t_i)

    sub = seg // SCAN_PARTS
    chains = [(ch, part) for ch in range(n_ch) for part in range(SCAN_PARTS)]

    def rows(ch, part, j):
        idx = part * sub + j if ch < n_slab else (part + 1) * sub - 1 - j
        return pl.ds(pl.multiple_of(idx * SEGMENTS, SEGMENTS), SEGMENTS)

    def summarize(j, carry):
        out = []
        for (ch, part), (h, p) in zip(chains, carry):
            a = a_scr[ch, rows(ch, part, j), :]
            u = u_scr[ch, rows(ch, part, j), :]
            out.append((a * h + u, p * a))
        return tuple(out)

    init = tuple((jnp.zeros(tile, F32), jnp.ones(tile, F32)) for _ in chains)
    ends = dict(zip(chains, lax.fori_loop(0, sub, summarize, init, unroll=SCAN_UNROLL)))

    starts = {}
    for ch in range(n_ch):
        forward = ch < n_slab
        pieces = [(sg, part) for sg in range(SEGMENTS) for part in range(SCAN_PARTS)]
        c = jnp.zeros((1, LRU_BLOCK), F32)
        entering = {}
        for sg, part in (pieces if forward else reversed(pieces)):
            entering[sg, part] = c
            h_end, p_end = ends[ch, part]
            c = h_end[sg:sg + 1, :] + p_end[sg:sg + 1, :] * c
        for part in range(SCAN_PARTS):
            starts[ch, part] = jnp.concatenate([entering[sg, part] for sg in range(SEGMENTS)], axis=0)

    def replay(j, carry):
        out = []
        for (ch, part), h_prev in zip(chains, carry):
            at = rows(ch, part, j)
            h = a_scr[ch, at, :] * h_prev + u_scr[ch, at, :]
            h_scr[ch, at, :] = h
            out.append(h)
        return tuple(out)

    lax.fori_loop(0, sub, replay, tuple(starts[c] for c in chains), unroll=SCAN_UNROLL)

    def scatter(j, carry):
        at = pl.ds(pl.multiple_of(j * SEGMENTS, SEGMENTS), SEGMENTS)
        for sl in range(n_slab):
            xs_scr[sl, pl.ds(j, SEGMENTS, stride=pitch), :] = h_scr[sl, at, :] + h_scr[n_slab + sl, at, :]
        return carry

    lax.fori_loop(0, seg, scatter, 0, unroll=SCAN_UNROLL)

    for sg in range(SEGMENTS):
        for sl in range(n_slab):
            h = xs_scr[sl, sg * pitch:sg * pitch + seg, :]
            gy = gy_ref[0, sg * seg:(sg + 1) * seg, lanes(sl)].astype(F32)
            o_ref[0, sg * seg:(sg + 1) * seg, lanes(sl)] = (h * gy).astype(BF16)


def _lru_call(xl, gy, conv_w, conv_b, wg, bg, lam):
    b, s, _ = xl.shape
    ct = LRU_CH_TILE
    n_slab = ct // LRU_BLOCK
    pitch = s // SEGMENTS + SEG_PAD
    seq_blk = pl.BlockSpec((1, s, ct), lambda bi, c: (bi, 0, c))
    work = pltpu.VMEM((2 * n_slab, s, LRU_BLOCK), F32)
    return pl.pallas_call(
        functools.partial(_lru_kernel, seq=s),
        grid=(b, D_MODEL // ct),
        in_specs=[seq_blk, seq_blk,
                  pl.BlockSpec((CONV_WIDTH, ct), lambda bi, c: (0, c)),
                  pl.BlockSpec((1, ct), lambda bi, c: (0, c)),
                  pl.BlockSpec((n_slab, LRU_BLOCK, 4 * LRU_BLOCK), lambda bi, c: (c, 0, 0)),
                  pl.BlockSpec((n_slab, 1, 4 * LRU_BLOCK), lambda bi, c: (c, 0, 0)),
                  pl.BlockSpec((2, ct), lambda bi, c: (0, c))],
        out_specs=seq_blk,
        out_shape=jax.ShapeDtypeStruct((b, s, D_MODEL), BF16),
        scratch_shapes=[pltpu.VMEM((n_slab, SEGMENTS * pitch, LRU_BLOCK), F32), work, work, work],
        compiler_params=_params(2),
        name="lru",
    )(xl, gy, conv_w, conv_b, wg, bg, lam)


def _merge_kernel(x_ref, att_ref, hl_ref, sa_ref, sl_ref, mod_ref, gpre_ref, gpost_ref,
                  wao_ref, wlo_ref, wout_ref, wgu_ref, wd_ref, o_ref):
    o_att = _dot(att_ref[...], wao_ref[...])
    o_lru = _dot(hl_ref[...], wlo_ref[...])
    merged = sa_ref[...].astype(F32) * o_att + sl_ref[...].astype(F32) * o_lru
    m = _rms(_dot(merged.astype(BF16), wout_ref[...]), gpost_ref[1:2, :])
    x = x_ref[...] + mod_ref[0, 5:6, :] * m
    o_ref[...] = _ffn_half_step(x, mod_ref, 2, gpre_ref, gpost_ref, wgu_ref, wd_ref)


def _merge_call(x2d, att, hl, sa, sl, mod, g_pre, g_post, wao, wlo, wout, wgu, wd, seq):
    n = x2d.shape[0]
    tm = TOKEN_TILE
    per_seq = seq // tm
    tok = lambda w: pl.BlockSpec((tm, w), lambda i: (i, 0))
    return pl.pallas_call(
        _merge_kernel,
        grid=(n // tm,),
        in_specs=[tok(D_MODEL), tok(N_HEADS * V_DIM), tok(D_MODEL), tok(D_MODEL), tok(D_MODEL),
                  pl.BlockSpec((1, 9, D_MODEL), lambda i: (i // per_seq, 0, 0)),
                  _resident(g_pre.shape), _resident(g_post.shape), _resident(wao.shape),
                  _resident(wlo.shape), _resident(wout.shape), _resident(wgu.shape),
                  _resident(wd.shape)],
        out_specs=tok(D_MODEL),
        out_shape=jax.ShapeDtypeStruct((n, D_MODEL), F32),
        compiler_params=_params(1),
        name="merge",
    )(x2d, att, hl, sa, sl, mod, g_pre, g_post, wao, wlo, wout, wgu, wd)


def _rope_partner(w):
    half = QK_ROPE // 2
    return jnp.concatenate([-w[:, half:], w[:, :half]], axis=1)


def _pad_rope_cols(w):
    rows = w.shape[0]
    return jnp.concatenate([jnp.zeros((rows, QK_NOPE), w.dtype), w,
                            jnp.zeros((rows, HEAD_PAD - QK_DIM), w.dtype)], axis=1)


def _prep_layer(w_in, w_q_b, w_kv_b, lru_w_a, lru_b_a, lru_w_i, lru_b_i):
    q0, q1, q2, q3, q4, q5 = (Q_LORA, Q_LORA + KV_LORA, Q_LORA + KV_LORA + QK_ROPE,
                              Q_LORA + KV_LORA + QK_ROPE + D_MODEL,
                              Q_LORA + KV_LORA + QK_ROPE + 2 * D_MODEL,
                              Q_LORA + KV_LORA + QK_ROPE + 3 * D_MODEL)
    w_kr = w_in[:, q1:q2]
    win = jnp.concatenate([w_in[:, :q1], _pad_rope_cols(w_kr), _pad_rope_cols(_rope_partner(w_kr)),
                           w_in[:, q2:q3], w_in[:, q3:q4], w_in[:, q4:q5], w_in[:, q5:]],
                          axis=1).astype(BF16)

    wq = w_q_b.reshape(Q_LORA, N_HEADS, QK_DIM)
    pad = jnp.zeros((Q_LORA, N_HEADS, HEAD_PAD - QK_DIM), w_q_b.dtype)
    zero_nope = jnp.zeros((Q_LORA, N_HEADS, QK_NOPE), w_q_b.dtype)
    rope = wq[:, :, QK_NOPE:]
    half = QK_ROPE // 2
    partner = jnp.concatenate([-rope[:, :, half:], rope[:, :, :half]], axis=2)
    wq_main = jnp.concatenate([wq, pad], axis=2).reshape(Q_LORA, N_HEADS * HEAD_PAD)
    wq_part = jnp.concatenate([zero_nope, partner, pad], axis=2).reshape(Q_LORA, N_HEADS * HEAD_PAD)
    wq2 = jnp.concatenate([wq_main, wq_part], axis=1).astype(BF16)

    wkv = w_kv_b.reshape(KV_LORA, N_HEADS, QK_NOPE + V_DIM)
    wk = jnp.concatenate([wkv[:, :, :QK_NOPE],
                          jnp.zeros((KV_LORA, N_HEADS, HEAD_PAD - QK_NOPE), w_kv_b.dtype)],
                         axis=2).reshape(KV_LORA, N_HEADS * HEAD_PAD).astype(BF16)
    wv = wkv[:, :, QK_NOPE:].reshape(KV_LORA, N_HEADS * V_DIM).T.astype(BF16)

    wg = (0.5 * jnp.concatenate([lru_w_a[0], lru_w_i[0], lru_w_a[1], lru_w_i[1]], axis=2)).astype(BF16)
    nblk = D_MODEL // LRU_BLOCK
    bg = 0.5 * jnp.concatenate([lru_b_a[0].reshape(nblk, 1, LRU_BLOCK), lru_b_i[0].reshape(nblk, 1, LRU_BLOCK),
                                lru_b_a[1].reshape(nblk, 1, LRU_BLOCK), lru_b_i[1].reshape(nblk, 1, LRU_BLOCK)],
                               axis=2)
    return win, wq2, wk, wv, wg, bg


def _rope_tables(seq):
    inv = 1.0 / (ROPE_THETA ** (jnp.arange(0, QK_ROPE, 2, dtype=F32) / QK_ROPE))
    ang = jnp.arange(seq, dtype=F32)[:, None] * inv[None, :]
    cos, sin = jnp.cos(ang), jnp.sin(ang)
    ones = jnp.ones((seq, QK_NOPE), F32)
    tail1 = jnp.ones((seq, HEAD_PAD - QK_DIM), F32)
    cos_t = jnp.concatenate([ones, cos, cos, tail1], axis=1)
    sin_t = jnp.concatenate([0.0 * ones, sin, sin, 0.0 * tail1], axis=1)
    return cos_t, sin_t


def _encoder_layer(x, mod, lw):
    b, s, _ = x.shape
    x2d = x.reshape(b * s, D_MODEL)
    cos_t, sin_t = _rope_tables(s)
    x1 = _ffn_call(x2d, mod, lw["g_pre"], lw["g_post"], lw["wgu1"], lw["wd1"], s)
    q, k, vt, xl, gy, sa, sl = _inproj_call(x1, mod, lw["g_pre"], lw["win"], lw["gq"], lw["gkv"],
                                           lw["wq2"], lw["wk"], lw["wv"], cos_t, sin_t, s)
    att = _attn_call(q.reshape(b, s, -1), k.reshape(b, s, -1), vt)
    hl = _lru_call(xl.reshape(b, s, D_MODEL), gy.reshape(b, s, D_MODEL), lw["conv_w"], lw["conv_b"],
                   lw["wg"], lw["bg"], lw["lam"])
    y = _merge_call(x1, att.reshape(b * s, -1), hl.reshape(b * s, D_MODEL), sa, sl, mod,
                    lw["g_pre"], lw["g_post"], lw["wao"], lw["wlo"], lw["wout"], lw["wgu2"],
                    lw["wd2"], s)
    return y.reshape(b, s, D_MODEL)


def kernel(x_prompt, x_sample, c_prompt, c_sample, w_ada, b_ada, g_pre, g_post, w_ffn1_gu, w_ffn1_down, w_in, g_q_norm, g_kv_norm, w_q_b, w_kv_b, w_attn_o, conv_w, conv_b, lru_w_a, lru_b_a, lru_w_i, lru_b_i, lru_lambda, w_lru_o, w_out, w_ffn2_gu, w_ffn2_down):
    depth = w_ada.shape[0]
    n_prompt = c_prompt.shape[0]
    y_prompt, y_sample = x_prompt, x_sample
    for l in range(depth):
        win, wq2, wk, wv, wg, bg = _prep_layer(w_in[l], w_q_b[l], w_kv_b[l], lru_w_a[l], lru_b_a[l],
                                               lru_w_i[l], lru_b_i[l])
        lw = dict(g_pre=g_pre[l], g_post=g_post[l],
                  wgu1=w_ffn1_gu[l].astype(BF16), wd1=w_ffn1_down[l].astype(BF16),
                  win=win, gq=g_q_norm[l].reshape(1, Q_LORA), gkv=g_kv_norm[l].reshape(1, KV_LORA),
                  wq2=wq2, wk=wk, wv=wv, conv_w=conv_w[l], conv_b=conv_b[l].reshape(1, D_MODEL),
                  wg=wg, bg=bg, lam=lru_lambda[l],
                  wao=w_attn_o[l].astype(BF16), wlo=w_lru_o[l].astype(BF16),
                  wout=w_out[l].astype(BF16),
                  wgu2=w_ffn2_gu[l].astype(BF16), wd2=w_ffn2_down[l].astype(BF16))
        c_all = jnp.concatenate([c_prompt, c_sample], axis=0)
        mod = _modulation(c_all, w_ada[l], b_ada[l]).reshape(c_all.shape[0], 9, D_MODEL)
        y_prompt = _encoder_layer(y_prompt, mod[:n_prompt], lw)
        y_sample = _encoder_layer(y_sample, mod[n_prompt:], lw)
    return (y_prompt, y_sample)
```

```python
import functools
import math

import jax
import jax.numpy as jnp
from jax import lax
from jax.experimental import pallas as pl
from jax.experimental.pallas import tpu as pltpu

F32 = jnp.float32
BF16 = jnp.bfloat16

D_MODEL = 1024
N_HEADS = 8
QK_NOPE = 64
QK_ROPE = 32
QK_DIM = QK_NOPE + QK_ROPE
V_DIM = 64
Q_LORA = 256
KV_LORA = 128
HEAD_PAD = 128
ROPE_THETA = 10000.0
LRU_BLOCK = 128
CONV_WIDTH = 4
LRU_C = 8.0
D_FF = 2816
EPS = 1e-6
FFN_RESIDUAL = 0.5

FF_CHUNK = 256
TOKEN_TILE = 512
Q_TILE = 2048
MERGE_ROW_GROUP = 256
Q_SUB = 512
ONES_ROWS = 16
LRU_CH_TILE = 256
SEGMENTS = 8
SEG_PAD = 8
SCAN_UNROLL = 8
SCAN_PARTS = 2
LRU_STEP_CHUNK = 64
VMEM_LIMIT = 56 * 1024 * 1024

_C_Q = 0
_C_KV = _C_Q + Q_LORA
_C_KR = _C_KV + KV_LORA
_C_KRP = _C_KR + HEAD_PAD
_C_X = _C_KRP + HEAD_PAD
_C_Y = _C_X + D_MODEL
_C_GA = _C_Y + D_MODEL
_C_GL = _C_GA + D_MODEL
_C_END = _C_GL + D_MODEL


def _params(n_axes, flags=None):
    return pltpu.CompilerParams(dimension_semantics=("parallel",) * n_axes,
                                vmem_limit_bytes=VMEM_LIMIT, flags=flags)


def _resident(shape):
    nd = len(shape)
    return pl.BlockSpec(shape, lambda *_: (0,) * nd, pipeline_mode=pl.Buffered(1))


def _rms(x, g):
    y = x * lax.rsqrt(jnp.mean(x * x, axis=-1, keepdims=True) + EPS)
    return y * g


def _dot(a, b):
    return jnp.dot(a, b, preferred_element_type=F32)


def _dot_nt(a, b):
    return lax.dot_general(a, b, (((1,), (1,)), ((), ())), preferred_element_type=F32)


def _sigmoid(x):
    return 0.5 * (1.0 + jnp.tanh(0.5 * x))


def _mod_kernel(c_ref, w_ref, b_ref, o_ref):
    c = c_ref[...]
    s = (c * _sigmoid(c)).astype(BF16)
    o_ref[...] = _dot(s, w_ref[...].astype(BF16)) + b_ref[...]


def _modulation(c, w_ada, b_ada):
    nb = c.shape[0]
    n = w_ada.shape[1]
    tn = 1024
    return pl.pallas_call(
        _mod_kernel,
        grid=(n // tn,),
        in_specs=[pl.BlockSpec((nb, D_MODEL), lambda j: (0, 0)),
                  pl.BlockSpec((D_MODEL, tn), lambda j: (0, j)),
                  pl.BlockSpec((1, tn), lambda j: (0, j))],
        out_specs=pl.BlockSpec((nb, tn), lambda j: (0, j)),
        out_shape=jax.ShapeDtypeStruct((nb, n), F32),
        compiler_params=_params(1),
        name="mod",
    )(c, w_ada, b_ada.reshape(1, n))


def _swiglu(h, wgu_ref, wd_ref):
    acc = None
    for c in range(0, D_FF, FF_CHUNK):
        g = _dot(h, wgu_ref[:, c:c + FF_CHUNK])
        u = _dot(h, wgu_ref[:, D_FF + c:D_FF + c + FF_CHUNK])
        a = (g * _sigmoid(g) * u).astype(BF16)
        d = _dot(a, wd_ref[c:c + FF_CHUNK, :])
        acc = d if acc is None else acc + d
    return acc


def _ffn_half_step(x, mod_ref, sub, gpre_ref, gpost_ref, wgu_ref, wd_ref):
    shift = mod_ref[0, 3 * sub:3 * sub + 1, :]
    scl = mod_ref[0, 3 * sub + 1:3 * sub + 2, :]
    gate = mod_ref[0, 3 * sub + 2:3 * sub + 3, :]
    h = _rms(x, gpre_ref[sub:sub + 1, :]) * (1.0 + scl) + shift
    f = _rms(_swiglu(h.astype(BF16), wgu_ref, wd_ref), gpost_ref[sub:sub + 1, :])
    return x + (FFN_RESIDUAL * gate) * f


def _ffn_kernel(x_ref, mod_ref, gpre_ref, gpost_ref, wgu_ref, wd_ref, o_ref):
    o_ref[...] = _ffn_half_step(x_ref[...], mod_ref, 0, gpre_ref, gpost_ref, wgu_ref, wd_ref)


def _ffn_call(x2d, mod, g_pre, g_post, wgu, wd, seq):
    n = x2d.shape[0]
    tm = TOKEN_TILE
    per_seq = seq // tm
    tok = pl.BlockSpec((tm, D_MODEL), lambda i: (i, 0))
    return pl.pallas_call(
        _ffn_kernel,
        grid=(n // tm,),
        in_specs=[tok,
                  pl.BlockSpec((1, 9, D_MODEL), lambda i: (i // per_seq, 0, 0)),
                  _resident(g_pre.shape), _resident(g_post.shape),
                  _resident(wgu.shape), _resident(wd.shape)],
        out_specs=tok,
        out_shape=jax.ShapeDtypeStruct((n, D_MODEL), F32),
        compiler_params=_params(1),
        name="ffn",
    )(x2d, mod, g_pre, g_post, wgu, wd)


def _inproj_kernel(x_ref, mod_ref, gpre_ref, win_ref, gq_ref, gkv_ref, wq_ref, wk_ref, wv_ref,
                   cos_ref, sin_ref,
                   q_ref, k_ref, vt_ref, xl_ref, gy_ref, sa_ref, sl_ref):
    x = x_ref[...]
    shift = mod_ref[0, 3:4, :]
    scl = mod_ref[0, 4:5, :]
    h = (_rms(x, gpre_ref[1:2, :]) * (1.0 + scl) + shift).astype(BF16)
    cos = cos_ref[...]
    sin = sin_ref[...]

    lat = _dot(h, win_ref[:, _C_Q:_C_X])
    cq_n = _rms(lat[:, _C_Q:_C_KV], gq_ref[...]).astype(BF16)
    ckv_n = _rms(lat[:, _C_KV:_C_KR], gkv_ref[...]).astype(BF16)

    k_rope = lat[:, _C_KR:_C_KRP] * cos + lat[:, _C_KRP:_C_X] * sin
    scale = QK_DIM ** -0.5 * math.log2(math.e)
    n_q = N_HEADS * HEAD_PAD
    q_all = _dot(cq_n, wq_ref[...])
    k_all = _dot(ckv_n, wk_ref[...])
    for hd in range(N_HEADS):
        lo = hd * HEAD_PAD
        q_h = q_all[:, lo:lo + HEAD_PAD]
        q_p = q_all[:, n_q + lo:n_q + lo + HEAD_PAD]
        q_ref[:, lo:lo + HEAD_PAD] = ((q_h * cos + q_p * sin) * scale).astype(BF16)
        k_ref[:, lo:lo + HEAD_PAD] = (k_all[:, lo:lo + HEAD_PAD] + k_rope).astype(BF16)
    vt_ref[0] = _dot_nt(wv_ref[...], ckv_n).astype(BF16)

    xl_ref[...] = _dot(h, win_ref[:, _C_X:_C_Y]).astype(BF16)
    gy_ref[...] = jax.nn.gelu(_dot(h, win_ref[:, _C_Y:_C_GA])).astype(BF16)
    sa_ref[...] = _sigmoid(_dot(h, win_ref[:, _C_GA:_C_GL])).astype(BF16)
    sl_ref[...] = _sigmoid(_dot(h, win_ref[:, _C_GL:_C_END])).astype(BF16)


def _inproj_call(x2d, mod, g_pre, win, gq, gkv, wq, wk, wv, cos_t, sin_t, seq):
    n = x2d.shape[0]
    tm = TOKEN_TILE
    per_seq = seq // tm
    tok = lambda w: pl.BlockSpec((tm, w), lambda i: (i, 0))
    pos = pl.BlockSpec((tm, HEAD_PAD), lambda i: (i % per_seq, 0))
    wide = N_HEADS * HEAD_PAD
    n_v = N_HEADS * V_DIM
    rows = lambda w: (pl.BlockSpec((tm, w), lambda i: (i, 0)), jax.ShapeDtypeStruct((n, w), BF16))
    v_t = (pl.BlockSpec((1, n_v, tm), lambda i: (i // per_seq, 0, i % per_seq)),
           jax.ShapeDtypeStruct((n // seq, n_v, seq), BF16))
    outs = [rows(wide), rows(wide), v_t] + [rows(D_MODEL)] * 4
    return pl.pallas_call(
        _inproj_kernel,
        grid=(n // tm,),
        in_specs=[tok(D_MODEL),
                  pl.BlockSpec((1, 9, D_MODEL), lambda i: (i // per_seq, 0, 0)),
                  _resident(g_pre.shape), _resident(win.shape), _resident(gq.shape),
                  _resident(gkv.shape), _resident(wq.shape), _resident(wk.shape),
                  _resident(wv.shape), pos, pos],
        out_specs=[spec for spec, _ in outs],
        out_shape=[shape for _, shape in outs],
        compiler_params=_params(1),
        name="inproj",
    )(x2d, mod, g_pre, win, gq, gkv, wq, wk, wv, cos_t, sin_t)


def _attn_kernel(q_ref, k_ref, vt_ref, o_ref):
    units = [(qt, hd) for qt in range(q_ref.shape[1] // Q_SUB) for hd in range(N_HEADS)]

    def scores(unit):
        qt, hd = unit
        lo = hd * HEAD_PAD
        q = q_ref[0, qt * Q_SUB:(qt + 1) * Q_SUB, lo:lo + HEAD_PAD]
        k = k_ref[0, :, lo:lo + HEAD_PAD]
        return _dot_nt(k, q)

    ones = jnp.ones((ONES_ROWS, k_ref.shape[1]), BF16)
    halves = []
    st_next = scores(units[0])
    for idx, (qt, hd) in enumerate(units):
        st = st_next
        if idx + 1 < len(units):
            st_next = scores(units[idx + 1])
        m = jnp.max(st, axis=0, keepdims=True)
        p = jnp.exp2(st - m).astype(BF16)
        vt = jnp.concatenate([vt_ref[0, hd * V_DIM:(hd + 1) * V_DIM, :], ones], axis=0)
        o = _dot(vt, p)
        halves.append(o[:V_DIM] / o[V_DIM:V_DIM + 1])
        if hd % 2 == 1:
            pair = hd // 2
            o_pair = jnp.concatenate(halves, axis=0).T
            o_ref[0, qt * Q_SUB:(qt + 1) * Q_SUB, pair * 2 * V_DIM:(pair + 1) * 2 * V_DIM] = (
                o_pair.astype(BF16))
            halves = []


def _attn_call(q, k, vt):
    b, s, _ = q.shape
    tq = min(Q_TILE, s)
    return pl.pallas_call(
        _attn_kernel,
        grid=(b, s // tq),
        in_specs=[pl.BlockSpec((1, tq, N_HEADS * HEAD_PAD), lambda bi, i: (bi, i, 0)),
                  pl.BlockSpec((1, s, N_HEADS * HEAD_PAD), lambda bi, i: (bi, 0, 0)),
                  pl.BlockSpec((1, N_HEADS * V_DIM, s), lambda bi, i: (bi, 0, 0))],
        out_specs=pl.BlockSpec((1, tq, N_HEADS * V_DIM), lambda bi, i: (bi, i, 0)),
        out_shape=jax.ShapeDtypeStruct((b, s, N_HEADS * V_DIM), BF16),
        compiler_params=_params(2),
        name="attn",
    )(q, k, vt)


def _lru_kernel(x_ref, gy_ref, cw_ref, cb_ref, wg_ref, bg_ref, lam_ref, o_ref,
                xs_scr, a_scr, u_scr, h_scr, *, seq):
    n_slab = LRU_CH_TILE // LRU_BLOCK
    n_ch = 2 * n_slab
    seg = seq // SEGMENTS
    pitch = seg + SEG_PAD
    tile = (SEGMENTS, LRU_BLOCK)

    def lanes(sl):
        return slice(sl * LRU_BLOCK, (sl + 1) * LRU_BLOCK)

    for sg in range(SEGMENTS):
        for sl in range(n_slab):
            xs_scr[sl, sg * pitch:sg * pitch + seg, :] = (
                x_ref[0, sg * seg:(sg + 1) * seg, lanes(sl)].astype(F32))

    lam = lam_ref[...]
    neg_lam = -lam
    softplus = jnp.maximum(neg_lam, 0.0) + jnp.log1p(jnp.exp(-jnp.abs(neg_lam)))
    decay = -LRU_C * softplus

    row = lax.broadcasted_iota(jnp.int32, tile, 0)

    def step_tile(sl, j):
        if j < 0:
            return jnp.where(row == 0, 0.0, pltpu.roll(step_tile(sl, seg + j), 1, axis=0))
        if j >= seg:
            return jnp.where(row == SEGMENTS - 1, 0.0,
                             pltpu.roll(step_tile(sl, j - seg), SEGMENTS - 1, axis=0))
        return xs_scr[sl, pl.ds(j, SEGMENTS, stride=pitch), :]

    left = (CONV_WIDTH - 1) // 2
    for c0 in range(0, seg, LRU_STEP_CHUNK):
        for sl in range(n_slab):
            tiles = {j: step_tile(sl, j)
                     for j in range(c0 - left, c0 + LRU_STEP_CHUNK + CONV_WIDTH - 1 - left)}
            xc = cb_ref[:, lanes(sl)]
            for t in range(CONV_WIDTH):
                tap = jnp.concatenate([tiles[j + t - left] for j in range(c0, c0 + LRU_STEP_CHUNK)],
                                      axis=0)
                xc = xc + tap * cw_ref[t:t + 1, lanes(sl)]
            g = _dot(xc.astype(BF16), wg_ref[sl]) + bg_ref[sl]
            half_x = 0.5 * xc
            span = slice(c0 * SEGMENTS, (c0 + LRU_STEP_CHUNK) * SEGMENTS)
            for d in range(2):
                t_r = jnp.tanh(g[:, (2 * d) * LRU_BLOCK:(2 * d + 1) * LRU_BLOCK])
                t_i = jnp.tanh(g[:, (2 * d + 1) * LRU_BLOCK:(2 * d + 2) * LRU_BLOCK])
                half_decay = 0.5 * decay[d:d + 1, lanes(sl)]
                log_a = half_decay + half_decay * t_r
                a = jnp.exp(log_a)
                w = -jnp.tanh(log_a) * (1.0 + a * a)
                root = jnp.where(w > 0.0, w * lax.rsqrt(w), 0.0)
                a_scr[d * n_slab + sl, span, :] = a
                u_scr[d * n_slab + sl, span, :] = root * (half_x + half_x * t_i)

    sub = seg // SCAN_PARTS
    chains = [(ch, part) for ch in range(n_ch) for part in range(SCAN_PARTS)]

    def rows(ch, part, j):
        idx = part * sub + j if ch < n_slab else (part + 1) * sub - 1 - j
        return pl.ds(pl.multiple_of(idx * SEGMENTS, SEGMENTS), SEGMENTS)

    def summarize(j, carry):
        out = []
        for (ch, part), (h, p) in zip(chains, carry):
            a = a_scr[ch, rows(ch, part, j), :]
            u = u_scr[ch, rows(ch, part, j), :]
            out.append((a * h + u, p * a))
        return tuple(out)

    init = tuple((jnp.zeros(tile, F32), jnp.ones(tile, F32)) for _ in chains)
    ends = dict(zip(chains, lax.fori_loop(0, sub, summarize, init, unroll=SCAN_UNROLL)))

    starts = {}
    for ch in range(n_ch):
        forward = ch < n_slab
        pieces = [(sg, part) for sg in range(SEGMENTS) for part in range(SCAN_PARTS)]
        c = jnp.zeros((1, LRU_BLOCK), F32)
        entering = {}
        for sg, part in (pieces if forward else reversed(pieces)):
            entering[sg, part] = c
            h_end, p_end = ends[ch, part]
            c = h_end[sg:sg + 1, :] + p_end[sg:sg + 1, :] * c
        for part in range(SCAN_PARTS):
            starts[ch, part] = jnp.concatenate([entering[sg, part] for sg in range(SEGMENTS)], axis=0)

    def replay(j, carry):
        out = []
        for (ch, part), h_prev in zip(chains, carry):
            at = rows(ch, part, j)
            h = a_scr[ch, at, :] * h_prev + u_scr[ch, at, :]
            h_scr[ch, at, :] = h
            out.append(h)
        return tuple(out)

    lax.fori_loop(0, sub, replay, tuple(starts[c] for c in chains), unroll=SCAN_UNROLL)

    def scatter(j, carry):
        at = pl.ds(pl.multiple_of(j * SEGMENTS, SEGMENTS), SEGMENTS)
        for sl in range(n_slab):
            xs_scr[sl, pl.ds(j, SEGMENTS, stride=pitch), :] = h_scr[sl, at, :] + h_scr[n_slab + sl, at, :]
        return carry

    lax.fori_loop(0, seg, scatter, 0, unroll=SCAN_UNROLL)

    for sg in range(SEGMENTS):
        for sl in range(n_slab):
            h = xs_scr[sl, sg * pitch:sg * pitch + seg, :]
            gy = gy_ref[0, sg * seg:(sg + 1) * seg, lanes(sl)].astype(F32)
            o_ref[0, sg * seg:(sg + 1) * seg, lanes(sl)] = (h * gy).astype(BF16)


def _lru_call(xl, gy, conv_w, conv_b, wg, bg, lam):
    b, s, _ = xl.shape
    ct = LRU_CH_TILE
    n_slab = ct // LRU_BLOCK
    pitch = s // SEGMENTS + SEG_PAD
    seq_blk = pl.BlockSpec((1, s, ct), lambda bi, c: (bi, 0, c))
    work = pltpu.VMEM((2 * n_slab, s, LRU_BLOCK), F32)
    return pl.pallas_call(
        functools.partial(_lru_kernel, seq=s),
        grid=(b, D_MODEL // ct),
        in_specs=[seq_blk, seq_blk,
                  pl.BlockSpec((CONV_WIDTH, ct), lambda bi, c: (0, c)),
                  pl.BlockSpec((1, ct), lambda bi, c: (0, c)),
                  pl.BlockSpec((n_slab, LRU_BLOCK, 4 * LRU_BLOCK), lambda bi, c: (c, 0, 0)),
                  pl.BlockSpec((n_slab, 1, 4 * LRU_BLOCK), lambda bi, c: (c, 0, 0)),
                  pl.BlockSpec((2, ct), lambda bi, c: (0, c))],
        out_specs=seq_blk,
        out_shape=jax.ShapeDtypeStruct((b, s, D_MODEL), BF16),
        scratch_shapes=[pltpu.VMEM((n_slab, SEGMENTS * pitch, LRU_BLOCK), F32), work, work, work],
        compiler_params=_params(2),
        name="lru",
    )(xl, gy, conv_w, conv_b, wg, bg, lam)


def _merge_kernel(x_ref, att_ref, hl_ref, sa_ref, sl_ref, mod_ref, gpre_ref, gpost_ref,
                  wao_ref, wlo_ref, wout_ref, wgu_ref, wd_ref, o_ref):
    parts = []
    for r0 in range(0, x_ref.shape[0], MERGE_ROW_GROUP):
        rows = slice(r0, r0 + MERGE_ROW_GROUP)
        o_att = _dot(att_ref[rows, :], wao_ref[...])
        o_lru = _dot(hl_ref[rows, :], wlo_ref[...])
        merged = sa_ref[rows, :].astype(F32) * o_att + sl_ref[rows, :].astype(F32) * o_lru
        m = _rms(_dot(merged.astype(BF16), wout_ref[...]), gpost_ref[1:2, :])
        parts.append(x_ref[rows, :] + mod_ref[0, 5:6, :] * m)
    x = jnp.concatenate(parts, axis=0)
    o_ref[...] = _ffn_half_step(x, mod_ref, 2, gpre_ref, gpost_ref, wgu_ref, wd_ref)


def _merge_call(x2d, att, hl, sa, sl, mod, g_pre, g_post, wao, wlo, wout, wgu, wd, seq):
    n = x2d.shape[0]
    tm = TOKEN_TILE
    per_seq = seq // tm
    tok = lambda w: pl.BlockSpec((tm, w), lambda i: (i, 0))
    return pl.pallas_call(
        _merge_kernel,
        grid=(n // tm,),
        in_specs=[tok(D_MODEL), tok(N_HEADS * V_DIM), tok(D_MODEL), tok(D_MODEL), tok(D_MODEL),
                  pl.BlockSpec((1, 9, D_MODEL), lambda i: (i // per_seq, 0, 0)),
                  _resident(g_pre.shape), _resident(g_post.shape), _resident(wao.shape),
                  _resident(wlo.shape), _resident(wout.shape), _resident(wgu.shape),
                  _resident(wd.shape)],
        out_specs=tok(D_MODEL),
        out_shape=jax.ShapeDtypeStruct((n, D_MODEL), F32),
        compiler_params=_params(1),
        name="merge",
    )(x2d, att, hl, sa, sl, mod, g_pre, g_post, wao, wlo, wout, wgu, wd)


def _rope_partner(w):
    half = QK_ROPE // 2
    return jnp.concatenate([-w[:, half:], w[:, :half]], axis=1)


def _pad_rope_cols(w):
    rows = w.shape[0]
    return jnp.concatenate([jnp.zeros((rows, QK_NOPE), w.dtype), w,
                            jnp.zeros((rows, HEAD_PAD - QK_DIM), w.dtype)], axis=1)


def _prep_layer(w_in, w_q_b, w_kv_b, lru_w_a, lru_b_a, lru_w_i, lru_b_i):
    q0, q1, q2, q3, q4, q5 = (Q_LORA, Q_LORA + KV_LORA, Q_LORA + KV_LORA + QK_ROPE,
                              Q_LORA + KV_LORA + QK_ROPE + D_MODEL,
                              Q_LORA + KV_LORA + QK_ROPE + 2 * D_MODEL,
                              Q_LORA + KV_LORA + QK_ROPE + 3 * D_MODEL)
    w_kr = w_in[:, q1:q2]
    win = jnp.concatenate([w_in[:, :q1], _pad_rope_cols(w_kr), _pad_rope_cols(_rope_partner(w_kr)),
                           w_in[:, q2:q3], w_in[:, q3:q4], w_in[:, q4:q5], w_in[:, q5:]],
                          axis=1).astype(BF16)

    wq = w_q_b.reshape(Q_LORA, N_HEADS, QK_DIM)
    pad = jnp.zeros((Q_LORA, N_HEADS, HEAD_PAD - QK_DIM), w_q_b.dtype)
    zero_nope = jnp.zeros((Q_LORA, N_HEADS, QK_NOPE), w_q_b.dtype)
    rope = wq[:, :, QK_NOPE:]
    half = QK_ROPE // 2
    partner = jnp.concatenate([-rope[:, :, half:], rope[:, :, :half]], axis=2)
    wq_main = jnp.concatenate([wq, pad], axis=2).reshape(Q_LORA, N_HEADS * HEAD_PAD)
    wq_part = jnp.concatenate([zero_nope, partner, pad], axis=2).reshape(Q_LORA, N_HEADS * HEAD_PAD)
    wq2 = jnp.concatenate([wq_main, wq_part], axis=1).astype(BF16)

    wkv = w_kv_b.reshape(KV_LORA, N_HEADS, QK_NOPE + V_DIM)
    wk = jnp.concatenate([wkv[:, :, :QK_NOPE],
                          jnp.zeros((KV_LORA, N_HEADS, HEAD_PAD - QK_NOPE), w_kv_b.dtype)],
                         axis=2).reshape(KV_LORA, N_HEADS * HEAD_PAD).astype(BF16)
    wv = wkv[:, :, QK_NOPE:].reshape(KV_LORA, N_HEADS * V_DIM).T.astype(BF16)

    wg = (0.5 * jnp.concatenate([lru_w_a[0], lru_w_i[0], lru_w_a[1], lru_w_i[1]], axis=2)).astype(BF16)
    nblk = D_MODEL // LRU_BLOCK
    bg = 0.5 * jnp.concatenate([lru_b_a[0].reshape(nblk, 1, LRU_BLOCK), lru_b_i[0].reshape(nblk, 1, LRU_BLOCK),
                                lru_b_a[1].reshape(nblk, 1, LRU_BLOCK), lru_b_i[1].reshape(nblk, 1, LRU_BLOCK)],
                               axis=2)
    return win, wq2, wk, wv, wg, bg


def _rope_tables(seq):
    inv = 1.0 / (ROPE_THETA ** (jnp.arange(0, QK_ROPE, 2, dtype=F32) / QK_ROPE))
    ang = jnp.arange(seq, dtype=F32)[:, None] * inv[None, :]
    cos, sin = jnp.cos(ang), jnp.sin(ang)
    ones = jnp.ones((seq, QK_NOPE), F32)
    tail1 = jnp.ones((seq, HEAD_PAD - QK_DIM), F32)
    cos_t = jnp.concatenate([ones, cos, cos, tail1], axis=1)
    sin_t = jnp.concatenate([0.0 * ones, sin, sin, 0.0 * tail1], axis=1)
    return cos_t, sin_t


def _encoder_layer(x, mod, lw):
    b, s, _ = x.shape
    x2d = x.reshape(b * s, D_MODEL)
    cos_t, sin_t = _rope_tables(s)
    x1 = _ffn_call(x2d, mod, lw["g_pre"], lw["g_post"], lw["wgu1"], lw["wd1"], s)
    q, k, vt, xl, gy, sa, sl = _inproj_call(x1, mod, lw["g_pre"], lw["win"], lw["gq"], lw["gkv"],
                                           lw["wq2"], lw["wk"], lw["wv"], cos_t, sin_t, s)
    att = _attn_call(q.reshape(b, s, -1), k.reshape(b, s, -1), vt)
    hl = _lru_call(xl.reshape(b, s, D_MODEL), gy.reshape(b, s, D_MODEL), lw["conv_w"], lw["conv_b"],
                   lw["wg"], lw["bg"], lw["lam"])
    y = _merge_call(x1, att.reshape(b * s, -1), hl.reshape(b * s, D_MODEL), sa, sl, mod,
                    lw["g_pre"], lw["g_post"], lw["wao"], lw["wlo"], lw["wout"], lw["wgu2"],
                    lw["wd2"], s)
    return y.reshape(b, s, D_MODEL)


def kernel(x_prompt, x_sample, c_prompt, c_sample, w_ada, b_ada, g_pre, g_post, w_ffn1_gu, w_ffn1_down, w_in, g_q_norm, g_kv_norm, w_q_b, w_kv_b, w_attn_o, conv_w, conv_b, lru_w_a, lru_b_a, lru_w_i, lru_b_i, lru_lambda, w_lru_o, w_out, w_ffn2_gu, w_ffn2_down):
    depth = w_ada.shape[0]
    n_prompt = c_prompt.shape[0]
    y_prompt, y_sample = x_prompt, x_sample
    for l in range(depth):
        win, wq2, wk, wv, wg, bg = _prep_layer(w_in[l], w_q_b[l], w_kv_b[l], lru_w_a[l], lru_b_a[l],
                                               lru_w_i[l], lru_b_i[l])
        lw = dict(g_pre=g_pre[l], g_post=g_post[l],
                  wgu1=w_ffn1_gu[l].astype(BF16), wd1=w_ffn1_down[l].astype(BF16),
                  win=win, gq=g_q_norm[l].reshape(1, Q_LORA), gkv=g_kv_norm[l].reshape(1, KV_LORA),
                  wq2=wq2, wk=wk, wv=wv, conv_w=conv_w[l], conv_b=conv_b[l].reshape(1, D_MODEL),
                  wg=wg, bg=bg, lam=lru_lambda[l],
                  wao=w_attn_o[l].astype(BF16), wlo=w_lru_o[l].astype(BF16),
                  wout=w_out[l].astype(BF16),
                  wgu2=w_ffn2_gu[l].astype(BF16), wd2=w_ffn2_down[l].astype(BF16))
        c_all = jnp.concatenate([c_prompt, c_sample], axis=0)
        mod = _modulation(c_all, w_ada[l], b_ada[l]).reshape(c_all.shape[0], 9, D_MODEL)
        y_prompt = _encoder_layer(y_prompt, mod[:n_prompt], lw)
        y_sample = _encoder_layer(y_sample, mod[n_prompt:], lw)
    return (y_prompt, y_sample)
```
